```python
import math
import jax, jax.numpy as jnp
from jax import lax
import numpy as np

D_MODEL = 2048
BATCH = 4
SEQ = 4096
DEPTH = 2

HEAD_DIM = 128
ATTN_HEADS = 12
ATTN_WIDTH = ATTN_HEADS * HEAD_DIM
DILATION_PATTERNS = ((128, 1), (512, 4), (2048, 16))
ATTN_BLOCK = 128
CONV_WIDTH = 1536
CONV_KERNEL = 3
RET_HEADS = 12
RET_HEAD_DIM = 128
RET_WIDTH = RET_HEADS * RET_HEAD_DIM
RET_CHUNK = 128
ROPE_BASE = 10000.0
NORM_EPS = 1e-6
IN_WIDTHS = (ATTN_WIDTH, ATTN_WIDTH, ATTN_WIDTH, ATTN_WIDTH,
             CONV_WIDTH, CONV_WIDTH, CONV_WIDTH, CONV_WIDTH,
             RET_WIDTH, RET_WIDTH, RET_WIDTH, RET_WIDTH,
             D_MODEL, D_MODEL, D_MODEL)
N_IN = sum(IN_WIDTHS)

kernel_name = "hybrid_dilated_conv_retention_gated_merge"


def rms_norm(x, g):
    xf = x.astype(jnp.float32)
    y = xf * lax.rsqrt(jnp.mean(xf * xf, axis=-1, keepdims=True) + NORM_EPS)
    return (y * g.astype(jnp.float32)).astype(x.dtype)


def dilated_pattern_attn(q, k, v, window, dil):
    B, S, H, E = q.shape
    n = S // dil
    w = window // dil
    blk = math.gcd(n, ATTN_BLOCK)
    nblk = n // blk
    qs = q.reshape(B, nblk, blk, dil, H, E)
    pad = ((0, 0), (w, 0), (0, 0), (0, 0), (0, 0))
    kp = jnp.pad(k.reshape(B, n, dil, H, E), pad)
    vp = jnp.pad(v.reshape(B, n, dil, H, E), pad)
    kidx = (jnp.arange(nblk) * blk)[:, None] + jnp.arange(blk + w)[None, :]
    kb = kp[:, kidx]
    vb = vp[:, kidx]
    qpos = (jnp.arange(nblk) * blk)[:, None] + jnp.arange(blk)[None, :]
    kpos = kidx - w
    dist = qpos[:, :, None] - kpos[:, None, :]
    mask = (kpos[:, None, :] >= 0) & (dist >= 0) & (dist <= w)
    s = jnp.einsum('bnqrhe,bnkrhe->bnrhqk', qs, kb) * (E ** -0.5)
    s = jnp.where(mask[None, :, None, None], s, -jnp.inf)
    m = jnp.max(s, axis=-1)
    p = jnp.exp(s - m[..., None])
    l = jnp.sum(p, axis=-1)
    o = jnp.einsum('bnrhqk,bnkrhe->bnqrhe', p, vb)
    m_t = m.transpose(0, 1, 4, 2, 3)
    l_t = l.transpose(0, 1, 4, 2, 3)
    o = (o / l_t[..., None]).reshape(B, S, H, E)
    lse = (m_t + jnp.log(l_t)).reshape(B, S, H)
    return o, lse


def dilated_mixture_attn(q, k, v):
    q = q.astype(jnp.float32); k = k.astype(jnp.float32); v = v.astype(jnp.float32)
    outs, lses = [], []
    for window, dil in DILATION_PATTERNS:
        o, lse = dilated_pattern_attn(q, k, v, window, dil)
        outs.append(o); lses.append(lse)
    alpha = jax.nn.softmax(jnp.stack(lses, 0), axis=0)
    return jnp.einsum('gbsh,gbshe->bshe', alpha, jnp.stack(outs, 0))


def causal_short_conv(u, w, b):
    K = w.shape[0]
    S = u.shape[1]
    up = jnp.pad(u, ((0, 0), (K - 1, 0), (0, 0)))
    y = b + w[K - 1] * u
    for j in range(K - 1):
        y = y + w[j] * up[:, j:j + S]
    return y


def rotary(x, pos):
    half = x.shape[-1] // 2
    inv_freq = ROPE_BASE ** (-jnp.arange(half, dtype=jnp.float32) / half)
    ang = pos[:, None] * inv_freq[None, :]
    cos = jnp.cos(ang)[None, :, None, :]
    sin = jnp.sin(ang)[None, :, None, :]
    x1, x2 = x[..., :half], x[..., half:]
    return jnp.concatenate([x1 * cos - x2 * sin, x1 * sin + x2 * cos], axis=-1)


def retention(q, k, v):
    B, S, H, E = q.shape
    C = RET_CHUNK
    nc = S // C
    pos = jnp.arange(S, dtype=jnp.float32)
    q = rotary(q.astype(jnp.float32), pos)
    k = rotary(k.astype(jnp.float32), pos) * (E ** -0.5)
    v = v.astype(jnp.float32)
    log_gamma = jnp.log1p(-jnp.exp2(-5.0 - jnp.arange(H, dtype=jnp.float32)))
    idx = jnp.arange(C, dtype=jnp.float32)
    rel = idx[:, None] - idx[None, :]
    decay_mask = jnp.where(rel[None] >= 0,
                           jnp.exp(jnp.maximum(rel, 0.0)[None] * log_gamma[:, None, None]), 0.0)
    q_decay = jnp.exp((idx + 1.0)[None, :] * log_gamma[:, None])[None, :, :, None]
    k_decay = jnp.exp((C - 1.0 - idx)[None, :] * log_gamma[:, None])[None, :, :, None]
    chunk_decay = jnp.exp(C * log_gamma)[None, :, None, None]

    def chunks(t):
        return t.reshape(B, nc, C, H, E).transpose(1, 0, 3, 2, 4)

    def step(state, inp):
        qi, ki, vi = inp
        inner = jnp.einsum('bhqe,bhke->bhqk', qi, ki) * decay_mask[None]
        o = jnp.einsum('bhqk,bhke->bhqe', inner, vi) \
            + jnp.einsum('bhqe,bhef->bhqf', qi, state) * q_decay
        state = state * chunk_decay + jnp.einsum('bhke,bhkf->bhef', ki * k_decay, vi)
        return state, o

    state0 = jnp.zeros((B, H, E, E), jnp.float32)
    _, o = lax.scan(step, state0, (chunks(q), chunks(k), chunks(v)))
    o = o.transpose(1, 0, 3, 2, 4).reshape(B, S, H, E)
    return o * lax.rsqrt(jnp.mean(o * o, axis=-1, keepdims=True) + NORM_EPS)


def hybrid_layer(x, pre_g, post_g, w_in, conv_w, conv_b, w_branch_a, w_branch_c, w_branch_r, w_out):
    B, S, _ = x.shape
    h = rms_norm(x, pre_g)
    proj = h @ w_in
    cuts, acc = [], 0
    for wdt in IN_WIDTHS[:-1]:
        acc += wdt
        cuts.append(acc)
    (q_a, k_a, v_a, z_a, u_c, b_c, c_c, z_c,
     q_r, k_r, v_r, z_r, g_a, g_c, g_r) = jnp.split(proj, cuts, axis=-1)

    ah = lambda t: t.reshape(B, S, ATTN_HEADS, HEAD_DIM)
    o_a = dilated_mixture_attn(ah(q_a), ah(k_a), ah(v_a)).reshape(B, S, ATTN_WIDTH).astype(x.dtype)
    o_a = o_a * jax.nn.silu(z_a)
    o_c = b_c * causal_short_conv(c_c * u_c, conv_w, conv_b) * jax.nn.silu(z_c)
    rh = lambda t: t.reshape(B, S, RET_HEADS, RET_HEAD_DIM)
    o_r = retention(rh(q_r), rh(k_r), rh(v_r)).reshape(B, S, RET_WIDTH).astype(x.dtype)
    o_r = o_r * jax.nn.silu(z_r)

    merged = (jax.nn.sigmoid(g_a) * (o_a @ w_branch_a)
              + jax.nn.sigmoid(g_c) * (o_c @ w_branch_c)
              + jax.nn.sigmoid(g_r) * (o_r @ w_branch_r))
    y = merged @ w_out
    return x + rms_norm(y, post_g)


def setup_inputs(seed: int = 0) -> dict:
    key = jax.random.key(seed)
    ks = jax.random.split(key, 11)
    f32 = jnp.float32
    x = jax.random.normal(ks[0], (BATCH, SEQ, D_MODEL), f32)
    pre_norm_g = 1.0 + 0.1 * jax.random.normal(ks[1], (DEPTH, D_MODEL), f32)
    post_norm_g = 1.0 + 0.1 * jax.random.normal(ks[2], (DEPTH, D_MODEL), f32)
    w_in = jax.random.normal(ks[3], (DEPTH, D_MODEL, N_IN), f32) * (D_MODEL ** -0.5)
    conv_w = jax.random.normal(ks[4], (DEPTH, CONV_KERNEL, CONV_WIDTH), f32) * (CONV_KERNEL ** -0.5)
    conv_b = 0.02 * jax.random.normal(ks[5], (DEPTH, CONV_WIDTH), f32)
    w_branch_a = jax.random.normal(ks[6], (DEPTH, ATTN_WIDTH, D_MODEL), f32) * (ATTN_WIDTH ** -0.5)
    w_branch_c = jax.random.normal(ks[7], (DEPTH, CONV_WIDTH, D_MODEL), f32) * (CONV_WIDTH ** -0.5)
    w_branch_r = jax.random.normal(ks[8], (DEPTH, RET_WIDTH, D_MODEL), f32) * (RET_WIDTH ** -0.5)
    w_out = jax.random.normal(ks[9], (DEPTH, D_MODEL, D_MODEL), f32) * (D_MODEL ** -0.5)
    return {"x": x, "pre_norm_g": pre_norm_g, "post_norm_g": post_norm_g, "w_in": w_in,
            "conv_w": conv_w, "conv_b": conv_b, "w_branch_a": w_branch_a,
            "w_branch_c": w_branch_c, "w_branch_r": w_branch_r, "w_out": w_out}


def reference(x, pre_norm_g, post_norm_g, w_in, conv_w, conv_b, w_branch_a, w_branch_c, w_branch_r, w_out):
    for layer in range(DEPTH):
        x = hybrid_layer(x, pre_norm_g[layer], post_norm_g[layer], w_in[layer],
                         conv_w[layer], conv_b[layer], w_branch_a[layer],
                         w_branch_c[layer], w_branch_r[layer], w_out[layer])
    return x
```

```python
import functools
import math

import jax
import jax.numpy as jnp
from jax import lax
from jax.experimental import pallas as pl
from jax.experimental.pallas import tpu as pltpu

F32 = jnp.float32
BF16 = jnp.bfloat16

LANE = 128
D_MODEL = 2048
HEAD_DIM = 128
N_HEADS = 12
WIDTH = N_HEADS * HEAD_DIM
N_IN = 12 * WIDTH + 3 * D_MODEL
N_SLABS = N_IN // LANE
DILATIONS = (1, 4, 16)
ATTN_TILE = 128
RET_CHUNK = 128
CONV_K = 3
ROPE_BASE = 10000.0
NORM_EPS = 1e-6
NEG_BIG = -1e30

SLAB_QA, SLAB_KA, SLAB_VA, SLAB_ZA = 0, 12, 24, 36
SLAB_UC, SLAB_BC, SLAB_CC, SLAB_ZC = 48, 60, 72, 84
SLAB_QR, SLAB_KR, SLAB_VR, SLAB_ZR = 96, 108, 120, 132
SLAB_GA, SLAB_GC, SLAB_GR = 144, 160, 176

VMEM_LIMIT = 56 * 1024 * 1024


def _silu(x):
    return x / (1.0 + jnp.exp(-x))


def _sigmoid(x):
    return 1.0 / (1.0 + jnp.exp(-x))


def _dot_nt(a, b):
    return lax.dot_general(a, b, (((1,), (1,)), ((), ())), preferred_element_type=F32)


def _dot(a, b):
    return jnp.dot(a, b, preferred_element_type=F32)


def _in_proj_kernel(x_ref, g_ref, w_ref, o_ref, h_ref):
    @pl.when(pl.program_id(1) == 0)
    def _():
        xf = x_ref[...]
        ms = jnp.mean(xf * xf, axis=-1, keepdims=True)
        h_ref[...] = (xf * lax.rsqrt(ms + NORM_EPS) * g_ref[...]).astype(BF16)

    acc = _dot(h_ref[...], w_ref[...])
    for c in range(o_ref.shape[0]):
        o_ref[c] = acc[:, c * LANE:(c + 1) * LANE].astype(BF16)


def _in_proj(x2, g, w_bf, tm=1024, tn=1024):
    m, d = x2.shape
    n = w_bf.shape[1]
    return pl.pallas_call(
        _in_proj_kernel,
        grid=(m // tm, n // tn),
        in_specs=[
            pl.BlockSpec((tm, d), lambda i, j: (i, 0)),
            pl.BlockSpec((1, d), lambda i, j: (0, 0)),
            pl.BlockSpec((d, tn), lambda i, j: (0, j)),
        ],
        out_specs=pl.BlockSpec((tn // LANE, tm, LANE), lambda i, j: (j, i, 0)),
        out_shape=jax.ShapeDtypeStruct((n // LANE, m, LANE), BF16),
        scratch_shapes=[pltpu.VMEM((tm, d), BF16)],
        compiler_params=pltpu.CompilerParams(
            dimension_semantics=("parallel", "arbitrary"), vmem_limit_bytes=VMEM_LIMIT),
        name="in_proj",
    )(x2, g.reshape(1, d), w_bf)


def _attn_kernel(q_ref, k_ref, v_ref, z_ref, o_ref, qf, kf, vf, o4, l4, o16, l16):
    seq = q_ref.shape[1]
    t = ATTN_TILE
    qf[...] = q_ref[0].astype(F32) * (HEAD_DIM ** -0.5)
    kf[...] = k_ref[0].astype(F32)
    vf[...] = v_ref[0].astype(F32)

    row = lax.broadcasted_iota(jnp.int32, (t, t), 0)
    col = lax.broadcasted_iota(jnp.int32, (t, t), 1)
    cur_ok = col <= row
    prev_ok = col >= row

    def rows(ref, start, dil):
        if dil == 1:
            return ref[pl.ds(pl.multiple_of(start, t), t), :]
        return ref[pl.ds(start, t, stride=dil), :]

    def window_attn(start, prev_start, has_prev, dil):
        q = rows(qf, start, dil).astype(BF16)
        s_c = _dot_nt(q, rows(kf, start, dil).astype(BF16))
        s_p = _dot_nt(q, rows(kf, prev_start, dil).astype(BF16))
        s_c = jnp.where(cur_ok, s_c, NEG_BIG)
        s_p = jnp.where(jnp.logical_and(prev_ok, has_prev), s_p, NEG_BIG)
        m = jnp.maximum(jnp.max(s_c, axis=-1, keepdims=True),
                        jnp.max(s_p, axis=-1, keepdims=True))
        p_c = jnp.exp(s_c - m)
        p_p = jnp.exp(s_p - m)
        l = jnp.sum(p_c, axis=-1, keepdims=True) + jnp.sum(p_p, axis=-1, keepdims=True)
        acc = (_dot(p_c.astype(BF16), rows(vf, start, dil).astype(BF16))
               + _dot(p_p.astype(BF16), rows(vf, prev_start, dil).astype(BF16)))
        return acc / l, m + jnp.log(l)

    def dilated_pass(dil, o_s, l_s):
        tiles_per_class = seq // (dil * t)

        def body(i, carry):
            r = i // tiles_per_class
            blk = i % tiles_per_class
            start = r + blk * (dil * t)
            prev_start = r + jnp.maximum(blk - 1, 0) * (dil * t)
            o, lse = window_attn(start, prev_start, blk > 0, dil)
            o_s[pl.ds(start, t, stride=dil), :] = o
            l_s[pl.ds(start, t, stride=dil), :] = jnp.broadcast_to(lse, (t, HEAD_DIM))
            return carry

        lax.fori_loop(0, seq // t, body, 0)

    dilated_pass(4, o4, l4)
    dilated_pass(16, o16, l16)

    def final_body(i, carry):
        start = pl.multiple_of(i * t, t)
        prev_start = jnp.maximum(i - 1, 0) * t
        o1, lse1 = window_attn(start, prev_start, i > 0, 1)
        sl = pl.ds(start, t)
        lse4 = l4[sl, :]
        lse16 = l16[sl, :]
        m = jnp.maximum(jnp.maximum(lse1, lse4), lse16)
        w1 = jnp.exp(lse1 - m)
        w4 = jnp.exp(lse4 - m)
        w16 = jnp.exp(lse16 - m)
        mix = (w1 * o1 + w4 * o4[sl, :] + w16 * o16[sl, :]) / (w1 + w4 + w16)
        o_ref[0, sl, :] = (mix * _silu(z_ref[0, sl, :].astype(F32))).astype(BF16)
        return carry

    lax.fori_loop(0, seq // t, final_body, 0)


def _attention(proj, batch, seq):
    m = proj.shape[1]

    def spec(base):
        return pl.BlockSpec((1, seq, LANE), lambda b, h: (base + h, b, 0))

    return pl.pallas_call(
        _attn_kernel,
        grid=(batch, N_HEADS),
        in_specs=[spec(SLAB_QA), spec(SLAB_KA), spec(SLAB_VA), spec(SLAB_ZA)],
        out_specs=pl.BlockSpec((1, seq, LANE), lambda b, h: (h, b, 0)),
        out_shape=jax.ShapeDtypeStruct((N_HEADS, m, LANE), BF16),
        scratch_shapes=[pltpu.VMEM((seq, LANE), F32) for _ in range(7)],
        compiler_params=pltpu.CompilerParams(
            dimension_semantics=("parallel", "parallel"), vmem_limit_bytes=VMEM_LIMIT),
        name="dilated_attn",
    )(proj, proj, proj, proj)


def _ret_kernel(lg_ref, q_ref, k_ref, v_ref, z_ref, cos_ref, sin_ref, o_ref):
    seq = q_ref.shape[1]
    c = RET_CHUNK
    lg = lg_ref[pl.program_id(1)]
    row = lax.broadcasted_iota(jnp.int32, (c, c), 0).astype(F32)
    col = lax.broadcasted_iota(jnp.int32, (c, c), 1).astype(F32)
    rel = row - col
    decay_mask = jnp.where(rel >= 0, jnp.exp(jnp.maximum(rel, 0.0) * lg), 0.0)
    q_decay = jnp.exp((row + 1.0) * lg)
    k_decay = jnp.exp((c - 1.0 - row) * lg)
    chunk_decay = jnp.exp(jnp.full((c, c), float(c), F32) * lg)

    def rotate(x, sl):
        return x * cos_ref[sl, :] + pltpu.roll(x, HEAD_DIM // 2, 1) * sin_ref[sl, :]

    def body(i, state):
        sl = pl.ds(pl.multiple_of(i * c, c), c)
        q = rotate(q_ref[0, sl, :].astype(F32), sl)
        k = rotate(k_ref[0, sl, :].astype(F32), sl) * (HEAD_DIM ** -0.5)
        v = v_ref[0, sl, :]
        qb = q.astype(BF16)
        inner = _dot_nt(qb, k.astype(BF16)) * decay_mask
        o = _dot(inner.astype(BF16), v) + _dot(qb, state.astype(BF16)) * q_decay
        kd_t = jnp.transpose(k * k_decay).astype(BF16)
        state = state * chunk_decay + _dot(kd_t, v)
        o = o * lax.rsqrt(jnp.mean(o * o, axis=-1, keepdims=True) + NORM_EPS)
        o_ref[0, sl, :] = (o * _silu(z_ref[0, sl, :].astype(F32))).astype(BF16)
        return state

    lax.fori_loop(0, seq // c, body, jnp.zeros((c, c), F32))


def _retention(proj, log_gamma, cos2, sin2, batch, seq):
    m = proj.shape[1]

    def spec(base):
        return pl.BlockSpec((1, seq, LANE), lambda b, h, lg: (base + h, b, 0))

    table = pl.BlockSpec((seq, LANE), lambda b, h, lg: (0, 0))
    return pl.pallas_call(
        _ret_kernel,
        grid_spec=pltpu.PrefetchScalarGridSpec(
            num_scalar_prefetch=1,
            grid=(batch, N_HEADS),
            in_specs=[spec(SLAB_QR), spec(SLAB_KR), spec(SLAB_VR), spec(SLAB_ZR), table, table],
            out_specs=pl.BlockSpec((1, seq, LANE), lambda b, h, lg: (h, b, 0)),
        ),
        out_shape=jax.ShapeDtypeStruct((N_HEADS, m, LANE), BF16),
        compiler_params=pltpu.CompilerParams(
            dimension_semantics=("parallel", "parallel"), vmem_limit_bytes=VMEM_LIMIT),
        name="retention",
    )(log_gamma, proj, proj, proj, proj, cos2, sin2)


def _merge_kernel(oa_ref, or_ref, u_ref, b_ref, c_ref, z_ref, uh_ref, ch_ref,
                  ga_ref, gc_ref, gr_ref, cw_ref, cb_ref, wa_ref, wc_ref, wr_ref,
                  o_ref, a_cat, c_cat, r_cat, *, tiles_per_seq):
    tm = oa_ref.shape[1]

    @pl.when(pl.program_id(1) == 0)
    def _():
        at_seq_start = (pl.program_id(0) % tiles_per_seq) == 0
        row = lax.broadcasted_iota(jnp.int32, (tm, LANE), 0)
        for h in range(N_HEADS):
            lanes = slice(h * LANE, (h + 1) * LANE)
            a_cat[:, lanes] = oa_ref[h]
            r_cat[:, lanes] = or_ref[h]
            cu = c_ref[h].astype(F32) * u_ref[h].astype(F32)
            halo = ch_ref[h].astype(F32) * uh_ref[h].astype(F32)
            halo = jnp.where(at_seq_start, 0.0, halo)
            back1 = jnp.where(row == 0, halo[7:8, :], pltpu.roll(cu, 1, 0))
            back2 = jnp.where(row == 0, halo[6:7, :],
                              jnp.where(row == 1, halo[7:8, :], pltpu.roll(cu, 2, 0)))
            conv = (cb_ref[h:h + 1, :] + cw_ref[2, h:h + 1, :] * cu
                    + cw_ref[0, h:h + 1, :] * back2 + cw_ref[1, h:h + 1, :] * back1)
            gated = b_ref[h].astype(F32) * conv * _silu(z_ref[h].astype(F32))
            c_cat[:, lanes] = gated.astype(BF16)

    ya = _dot(a_cat[...], wa_ref[...])
    yc = _dot(c_cat[...], wc_ref[...])
    yr = _dot(r_cat[...], wr_ref[...])
    for s in range(ga_ref.shape[0]):
        lanes = slice(s * LANE, (s + 1) * LANE)
        merged = (_sigmoid(ga_ref[s].astype(F32)) * ya[:, lanes]
                  + _sigmoid(gc_ref[s].astype(F32)) * yc[:, lanes]
                  + _sigmoid(gr_ref[s].astype(F32)) * yr[:, lanes])
        o_ref[:, lanes] = merged.astype(BF16)


def _merge(proj, oa, orr, conv_w, conv_b, wa, wc, wr, seq, tm=512, tn=512):
    m = proj.shape[1]
    gs = tn // LANE
    halo_rows = 8
    hb = tm // halo_rows

    def branch(base):
        return pl.BlockSpec((N_HEADS, tm, LANE), lambda i, n: (base // N_HEADS, i, 0))

    def halo(base):
        return pl.BlockSpec((N_HEADS, halo_rows, LANE),
                            lambda i, n: (base // N_HEADS, jnp.maximum(i * hb - 1, 0), 0))

    def gate(base):
        return pl.BlockSpec((gs, tm, LANE), lambda i, n: (base // gs + n, i, 0))

    wspec = pl.BlockSpec((WIDTH, tn), lambda i, n: (0, n))
    own = pl.BlockSpec((N_HEADS, tm, LANE), lambda i, n: (0, i, 0))
    return pl.pallas_call(
        functools.partial(_merge_kernel, tiles_per_seq=seq // tm),
        grid=(m // tm, D_MODEL // tn),
        in_specs=[own, own, branch(SLAB_UC), branch(SLAB_BC), branch(SLAB_CC), branch(SLAB_ZC),
                  halo(SLAB_UC), halo(SLAB_CC), gate(SLAB_GA), gate(SLAB_GC), gate(SLAB_GR),
                  pl.BlockSpec((CONV_K, N_HEADS, LANE), lambda i, n: (0, 0, 0)),
                  pl.BlockSpec((N_HEADS, LANE), lambda i, n: (0, 0)),
                  wspec, wspec, wspec],
        out_specs=pl.BlockSpec((tm, tn), lambda i, n: (i, n)),
        out_shape=jax.ShapeDtypeStruct((m, D_MODEL), BF16),
        scratch_shapes=[pltpu.VMEM((tm, WIDTH), BF16) for _ in range(3)],
        compiler_params=pltpu.CompilerParams(
            dimension_semantics=("parallel", "arbitrary"), vmem_limit_bytes=VMEM_LIMIT),
        name="branch_merge",
    )(oa, orr, proj, proj, proj, proj, proj, proj, proj, proj, proj,
      conv_w.reshape(CONV_K, N_HEADS, LANE), conv_b.reshape(N_HEADS, LANE), wa, wc, wr)


def _out_proj_kernel(m_ref, w_ref, x_ref, g_ref, o_ref):
    y = _dot(m_ref[...], w_ref[...])
    ms = jnp.mean(y * y, axis=-1, keepdims=True)
    o_ref[...] = x_ref[...] + y * lax.rsqrt(ms + NORM_EPS) * g_ref[...]


def _out_proj(merged, w_bf, x2, g, tm=512):
    m, d = x2.shape
    return pl.pallas_call(
        _out_proj_kernel,
        grid=(m // tm,),
        in_specs=[pl.BlockSpec((tm, d), lambda i: (i, 0)),
                  pl.BlockSpec((d, d), lambda i: (0, 0)),
                  pl.BlockSpec((tm, d), lambda i: (i, 0)),
                  pl.BlockSpec((1, d), lambda i: (0, 0))],
        out_specs=pl.BlockSpec((tm, d), lambda i: (i, 0)),
        out_shape=jax.ShapeDtypeStruct((m, d), F32),
        compiler_params=pltpu.CompilerParams(
            dimension_semantics=("parallel",), vmem_limit_bytes=VMEM_LIMIT),
        name="out_proj",
    )(merged, w_bf, x2, g.reshape(1, d))


def _rotary_tables(seq):
    half = HEAD_DIM // 2
    inv_freq = ROPE_BASE ** (-jnp.arange(half, dtype=F32) / half)
    ang = jnp.arange(seq, dtype=F32)[:, None] * inv_freq[None, :]
    cos, sin = jnp.cos(ang), jnp.sin(ang)
    return jnp.concatenate([cos, cos], axis=-1), jnp.concatenate([-sin, sin], axis=-1)


def kernel(x, pre_norm_g, post_norm_g, w_in, conv_w, conv_b, w_branch_a, w_branch_c, w_branch_r, w_out):
    batch, seq, d = x.shape
    assert d == D_MODEL and w_in.shape[-1] == N_IN
    assert seq % (max(DILATIONS) * ATTN_TILE) == 0 and seq % RET_CHUNK == 0
    cos2, sin2 = _rotary_tables(seq)
    log_gamma = jnp.log1p(-jnp.exp2(-5.0 - jnp.arange(N_HEADS, dtype=F32)))
    x2 = x.reshape(batch * seq, d)
    for layer in range(w_in.shape[0]):
        proj = _in_proj(x2, pre_norm_g[layer], w_in[layer].astype(BF16))
        oa = _attention(proj, batch, seq)
        orr = _retention(proj, log_gamma, cos2, sin2, batch, seq)
        merged = _merge(proj, oa, orr, conv_w[layer], conv_b[layer],
                        w_branch_a[layer].astype(BF16), w_branch_c[layer].astype(BF16),
                        w_branch_r[layer].astype(BF16), seq)
        x2 = _out_proj(merged, w_out[layer].astype(BF16), x2, post_norm_g[layer])
    return x2.reshape(batch, seq, d)
```

```python
import functools
import math

import jax
import jax.numpy as jnp
from jax import lax
from jax.experimental import pallas as pl
from jax.experimental.pallas import tpu as pltpu

F32 = jnp.float32
BF16 = jnp.bfloat16

LANE = 128
D_MODEL = 2048
HEAD_DIM = 128
N_HEADS = 12
WIDTH = N_HEADS * HEAD_DIM
N_IN = 12 * WIDTH + 3 * D_MODEL
N_SLABS = N_IN // LANE
DILATIONS = (1, 4, 16)
ATTN_TILE = 128
ATTN_BATCH = 8
LOG2_E = math.log2(math.e)
RET_CHUNK = 128
RET_BATCH = 8
CONV_K = 3
ROPE_BASE = 10000.0
NORM_EPS = 1e-6
NEG_BIG = -1e30

SLAB_QA, SLAB_KA, SLAB_VA, SLAB_ZA = 0, 12, 24, 36
SLAB_UC, SLAB_BC, SLAB_CC, SLAB_ZC = 48, 60, 72, 84
SLAB_QR, SLAB_KR, SLAB_VR, SLAB_ZR = 96, 108, 120, 132
SLAB_GA, SLAB_GC, SLAB_GR = 144, 160, 176

VMEM_LIMIT = 56 * 1024 * 1024


def _silu(x):
    return x / (1.0 + jnp.exp(-x))


def _sigmoid(x):
    return 1.0 / (1.0 + jnp.exp(-x))


def _dot_nt(a, b):
    return lax.dot_general(a, b, (((1,), (1,)), ((), ())), preferred_element_type=F32)


def _dot(a, b):
    return jnp.dot(a, b, preferred_element_type=F32)


def _in_proj_kernel(x_ref, g_ref, w_ref, o_ref, h_ref):
    @pl.when(pl.program_id(1) == 0)
    def _():
        xf = x_ref[...]
        ms = jnp.mean(xf * xf, axis=-1, keepdims=True)
        h_ref[...] = (xf * lax.rsqrt(ms + NORM_EPS) * g_ref[...]).astype(BF16)

    acc = _dot(h_ref[...], w_ref[...])
    for c in range(o_ref.shape[0]):
        o_ref[c] = acc[:, c * LANE:(c + 1) * LANE].astype(BF16)


def _in_proj(x2, g, w_bf, tm=1024, tn=1024):
    m, d = x2.shape
    n = w_bf.shape[1]
    return pl.pallas_call(
        _in_proj_kernel,
        grid=(m // tm, n // tn),
        in_specs=[
            pl.BlockSpec((tm, d), lambda i, j: (i, 0)),
            pl.BlockSpec((1, d), lambda i, j: (0, 0)),
            pl.BlockSpec((d, tn), lambda i, j: (0, j)),
        ],
        out_specs=pl.BlockSpec((tn // LANE, tm, LANE), lambda i, j: (j, i, 0)),
        out_shape=jax.ShapeDtypeStruct((n // LANE, m, LANE), BF16),
        scratch_shapes=[pltpu.VMEM((tm, d), BF16)],
        compiler_params=pltpu.CompilerParams(
            dimension_semantics=("parallel", "arbitrary"), vmem_limit_bytes=VMEM_LIMIT),
        name="in_proj",
    )(x2, g.reshape(1, d), w_bf)


def _attn_kernel(q_ref, k_ref, v_ref, z_ref, o_ref, qf, kf, vf, a4, m4, l4, a16, m16, l16):
    seq = q_ref.shape[1]
    t = ATTN_TILE
    g = ATTN_BATCH
    e = HEAD_DIM
    pad = kf.shape[0] - seq
    qf[...] = q_ref[0].astype(F32) * (HEAD_DIM ** -0.5 * LOG2_E)
    kf[pl.ds(0, pad), :] = jnp.zeros((pad, e), F32)
    vf[pl.ds(0, pad), :] = jnp.zeros((pad, e), F32)
    kf[pl.ds(pad, seq), :] = k_ref[0].astype(F32)
    vf[pl.ds(pad, seq), :] = v_ref[0].astype(F32)

    row = lax.broadcasted_iota(jnp.int32, (g, t, t), 1)
    col = lax.broadcasted_iota(jnp.int32, (g, t, t), 2)
    slot = lax.broadcasted_iota(jnp.int32, (g, t, t), 0)
    cur_ok = col <= row
    prev_ok = col >= row

    def bdot(a, b, contract_b):
        return lax.dot_general(a, b, (((2,), (contract_b,)), ((0,), (0,))),
                               preferred_element_type=F32)

    def strided(ref, start, n, dil):
        if dil == 1:
            return ref[pl.ds(pl.multiple_of(start, t), n), :]
        return ref[pl.ds(start, n, stride=dil), :]

    def partials(segments, tile_idx, dil):
        qs, kps, kcs, vps, vcs = [], [], [], [], []
        for start, n in segments:
            qs.append(strided(qf, start, n * t, dil).astype(BF16).reshape(n, t, e))
            k_all = strided(kf, pad + start - dil * t, (n + 1) * t, dil).astype(BF16)
            v_all = strided(vf, pad + start - dil * t, (n + 1) * t, dil).astype(BF16)
            v_all = jnp.concatenate([v_all, jnp.ones(v_all.shape, BF16)], axis=1)
            kps.append(k_all[:n * t].reshape(n, t, e))
            kcs.append(k_all[t:].reshape(n, t, e))
            vps.append(v_all[:n * t].reshape(n, t, 2 * e))
            vcs.append(v_all[t:].reshape(n, t, 2 * e))
        cat = lambda xs: xs[0] if len(xs) == 1 else jnp.concatenate(xs, axis=0)
        q = cat(qs)
        s_c = jnp.where(cur_ok, bdot(q, cat(kcs), 2), NEG_BIG)
        s_p = jnp.where(jnp.logical_and(prev_ok, tile_idx > 0), bdot(q, cat(kps), 2), NEG_BIG)
        m = jnp.max(jnp.maximum(s_c, s_p), axis=-1, keepdims=True)
        p_c = jnp.exp2(s_c - m).astype(BF16)
        p_p = jnp.exp2(s_p - m).astype(BF16)
        acc_l = bdot(p_c, cat(vcs), 1) + bdot(p_p, cat(vps), 1)
        return acc_l[:, :, :e], m, acc_l[:, :, e:]

    def dilated_pass(dil, a_s, m_s, l_s):
        tiles_per_class = seq // (dil * t)
        if g <= tiles_per_class:
            batches_per_class = tiles_per_class // g

            def layout(i):
                blk0 = (i % batches_per_class) * g
                start = i // batches_per_class + blk0 * (dil * t)
                return [(start, g)], slot + blk0
        else:
            classes_per_batch = g // tiles_per_class

            def layout(i):
                return ([(i * classes_per_batch + c, tiles_per_class)
                         for c in range(classes_per_batch)], slot % tiles_per_class)

        def body(i, carry):
            segments, tile_idx = layout(i)
            acc, m, l = partials(segments, tile_idx, dil)
            m = jnp.broadcast_to(m, acc.shape)
            done = 0
            for start, n in segments:
                dst = pl.ds(start, n * t, stride=dil)
                a_s[dst, :] = acc[done:done + n].reshape(n * t, e)
                m_s[dst, :] = m[done:done + n].reshape(n * t, e)
                l_s[dst, :] = l[done:done + n].reshape(n * t, e)
                done += n
            return carry

        lax.fori_loop(0, seq // (g * t), body, 0)

    dilated_pass(4, a4, m4, l4)
    dilated_pass(16, a16, m16, l16)

    def final_body(i, carry):
        start = pl.multiple_of(i * (g * t), g * t)
        acc1, m1, l1 = partials([(start, g)], slot + i * g, 1)
        sl = pl.ds(start, g * t)
        load = lambda ref: ref[sl, :].reshape(g, t, e)
        mm4, mm16 = load(m4), load(m16)
        m = jnp.maximum(jnp.maximum(m1, mm4), mm16)
        w1 = jnp.exp2(m1 - m)
        w4 = jnp.exp2(mm4 - m)
        w16 = jnp.exp2(mm16 - m)
        num = w1 * acc1 + w4 * load(a4) + w16 * load(a16)
        den = w1 * l1 + w4 * load(l4) + w16 * load(l16)
        z = z_ref[0, sl, :].astype(F32).reshape(g, t, e)
        o_ref[0, sl, :] = (num / den * _silu(z)).astype(BF16).reshape(g * t, e)
        return carry

    lax.fori_loop(0, seq // (g * t), final_body, 0)


def _attention(proj, batch, seq):
    m = proj.shape[1]

    def spec(base):
        return pl.BlockSpec((1, seq, LANE), lambda b, h: (base + h, b, 0))

    return pl.pallas_call(
        _attn_kernel,
        grid=(batch, N_HEADS),
        in_specs=[spec(SLAB_QA), spec(SLAB_KA), spec(SLAB_VA), spec(SLAB_ZA)],
        out_specs=pl.BlockSpec((1, seq, LANE), lambda b, h: (h, b, 0)),
        out_shape=jax.ShapeDtypeStruct((N_HEADS, m, LANE), BF16),
        scratch_shapes=([pltpu.VMEM((seq, LANE), F32)]
                        + [pltpu.VMEM((seq + max(DILATIONS) * ATTN_TILE, LANE), F32)] * 2
                        + [pltpu.VMEM((seq, LANE), F32)] * 6),
        compiler_params=pltpu.CompilerParams(
            dimension_semantics=("parallel", "parallel"), vmem_limit_bytes=VMEM_LIMIT),
        name="dilated_attn",
    )(proj, proj, proj, proj)


def _ret_kernel(lg_ref, q_ref, k_ref, v_ref, z_ref, cos_ref, sin_ref, o_ref):
    seq = q_ref.shape[1]
    c = RET_CHUNK
    lg = lg_ref[pl.program_id(1)]
    row = lax.broadcasted_iota(jnp.int32, (c, c), 0).astype(F32)
    col = lax.broadcasted_iota(jnp.int32, (c, c), 1).astype(F32)
    rel = row - col
    decay_mask = jnp.where(rel >= 0, jnp.exp(jnp.maximum(rel, 0.0) * lg), 0.0)
    q_decay = jnp.exp((row + 1.0) * lg)
    k_decay = jnp.exp((c - 1.0 - row) * lg)
    chunk_decay = jnp.exp(jnp.full((c, c), float(c), F32) * lg)

    g = RET_BATCH
    e = HEAD_DIM

    def bdot(a, b, contract_a, contract_b):
        return lax.dot_general(a, b, (((contract_a,), (contract_b,)), ((0,), (0,))),
                               preferred_element_type=F32)

    def body(i, state):
        sl = pl.ds(pl.multiple_of(i * (g * c), g * c), g * c)
        cos, sin = cos_ref[sl, :], sin_ref[sl, :]
        rotate = lambda x: x * cos + pltpu.roll(x, e // 2, 1) * sin
        q = rotate(q_ref[0, sl, :].astype(F32)).reshape(g, c, e)
        k = (rotate(k_ref[0, sl, :].astype(F32)) * (HEAD_DIM ** -0.5)).reshape(g, c, e)
        v = v_ref[0, sl, :].reshape(g, c, e)
        qb = q.astype(BF16)
        inner = bdot(qb, k.astype(BF16), 2, 2) * decay_mask
        o = bdot(inner.astype(BF16), v, 2, 1)
        kv = bdot((k * k_decay).astype(BF16), v, 1, 1)
        states = []
        for j in range(g):
            states.append(state)
            state = state * chunk_decay + kv[j]
        o = o + bdot(qb, jnp.stack(states).astype(BF16), 2, 1) * q_decay
        o = o * lax.rsqrt(jnp.mean(o * o, axis=-1, keepdims=True) + NORM_EPS)
        z = z_ref[0, sl, :].astype(F32).reshape(g, c, e)
        o_ref[0, sl, :] = (o * _silu(z)).astype(BF16).reshape(g * c, e)
        return state

    lax.fori_loop(0, seq // (g * c), body, jnp.zeros((e, e), F32))


def _retention(proj, log_gamma, cos2, sin2, batch, seq):
    m = proj.shape[1]

    def spec(base):
        return pl.BlockSpec((1, seq, LANE), lambda b, h, lg: (base + h, b, 0))

    table = pl.BlockSpec((seq, LANE), lambda b, h, lg: (0, 0))
    return pl.pallas_call(
        _ret_kernel,
        grid_spec=pltpu.PrefetchScalarGridSpec(
            num_scalar_prefetch=1,
            grid=(batch, N_HEADS),
            in_specs=[spec(SLAB_QR), spec(SLAB_KR), spec(SLAB_VR), spec(SLAB_ZR), table, table],
            out_specs=pl.BlockSpec((1, seq, LANE), lambda b, h, lg: (h, b, 0)),
        ),
        out_shape=jax.ShapeDtypeStruct((N_HEADS, m, LANE), BF16),
        compiler_params=pltpu.CompilerParams(
            dimension_semantics=("parallel", "parallel"), vmem_limit_bytes=VMEM_LIMIT),
        name="retention",
    )(log_gamma, proj, proj, proj, proj, cos2, sin2)


def _merge_kernel(oa_ref, or_ref, u_ref, b_ref, c_ref, z_ref, uh_ref, ch_ref,
                  ga_ref, gc_ref, gr_ref, cw_ref, cb_ref, wa_ref, wc_ref, wr_ref,
                  o_ref, a_cat, c_cat, r_cat, *, tiles_per_seq):
    tm = oa_ref.shape[1]

    @pl.when(pl.program_id(1) == 0)
    def _():
        at_seq_start = (pl.program_id(0) % tiles_per_seq) == 0
        row = lax.broadcasted_iota(jnp.int32, (tm, LANE), 0)
        for h in range(N_HEADS):
            lanes = slice(h * LANE, (h + 1) * LANE)
            a_cat[:, lanes] = oa_ref[h]
            r_cat[:, lanes] = or_ref[h]
            cu = c_ref[h].astype(F32) * u_ref[h].astype(F32)
            halo = ch_ref[h].astype(F32) * uh_ref[h].astype(F32)
            halo = jnp.where(at_seq_start, 0.0, halo)
            back1 = jnp.where(row == 0, halo[7:8, :], pltpu.roll(cu, 1, 0))
            back2 = jnp.where(row == 0, halo[6:7, :],
                              jnp.where(row == 1, halo[7:8, :], pltpu.roll(cu, 2, 0)))
            conv = (cb_ref[h:h + 1, :] + cw_ref[2, h:h + 1, :] * cu
                    + cw_ref[0, h:h + 1, :] * back2 + cw_ref[1, h:h + 1, :] * back1)
            gated = b_ref[h].astype(F32) * conv * _silu(z_ref[h].astype(F32))
            c_cat[:, lanes] = gated.astype(BF16)

    ya = _dot(a_cat[...], wa_ref[...])
    yc = _dot(c_cat[...], wc_ref[...])
    yr = _dot(r_cat[...], wr_ref[...])
    for s in range(ga_ref.shape[0]):
        lanes = slice(s * LANE, (s + 1) * LANE)
        merged = (_sigmoid(ga_ref[s].astype(F32)) * ya[:, lanes]
                  + _sigmoid(gc_ref[s].astype(F32)) * yc[:, lanes]
                  + _sigmoid(gr_ref[s].astype(F32)) * yr[:, lanes])
        o_ref[:, lanes] = merged.astype(BF16)


def _merge(proj, oa, orr, conv_w, conv_b, wa, wc, wr, seq, tm=512, tn=512):
    m = proj.shape[1]
    gs = tn // LANE
    halo_rows = 8
    hb = tm // halo_rows

    def branch(base):
        return pl.BlockSpec((N_HEADS, tm, LANE), lambda i, n: (base // N_HEADS, i, 0))

    def halo(base):
        return pl.BlockSpec((N_HEADS, halo_rows, LANE),
                            lambda i, n: (base // N_HEADS, jnp.maximum(i * hb - 1, 0), 0))

    def gate(base):
        return pl.BlockSpec((gs, tm, LANE), lambda i, n: (base // gs + n, i, 0))

    wspec = pl.BlockSpec((WIDTH, tn), lambda i, n: (0, n))
    own = pl.BlockSpec((N_HEADS, tm, LANE), lambda i, n: (0, i, 0))
    return pl.pallas_call(
        functools.partial(_merge_kernel, tiles_per_seq=seq // tm),
        grid=(m // tm, D_MODEL // tn),
        in_specs=[own, own, branch(SLAB_UC), branch(SLAB_BC), branch(SLAB_CC), branch(SLAB_ZC),
                  halo(SLAB_UC), halo(SLAB_CC), gate(SLAB_GA), gate(SLAB_GC), gate(SLAB_GR),
                  pl.BlockSpec((CONV_K, N_HEADS, LANE), lambda i, n: (0, 0, 0)),
                  pl.BlockSpec((N_HEADS, LANE), lambda i, n: (0, 0)),
                  wspec, wspec, wspec],
        out_specs=pl.BlockSpec((tm, tn), lambda i, n: (i, n)),
        out_shape=jax.ShapeDtypeStruct((m, D_MODEL), BF16),
        scratch_shapes=[pltpu.VMEM((tm, WIDTH), BF16) for _ in range(3)],
        compiler_params=pltpu.CompilerParams(
            dimension_semantics=("parallel", "arbitrary"), vmem_limit_bytes=VMEM_LIMIT),
        name="branch_merge",
    )(oa, orr, proj, proj, proj, proj, proj, proj, proj, proj, proj,
      conv_w.reshape(CONV_K, N_HEADS, LANE), conv_b.reshape(N_HEADS, LANE), wa, wc, wr)


def _out_proj_kernel(m_ref, w_ref, x_ref, g_ref, o_ref):
    y = _dot(m_ref[...], w_ref[...])
    ms = jnp.mean(y * y, axis=-1, keepdims=True)
    o_ref[...] = x_ref[...] + y * lax.rsqrt(ms + NORM_EPS) * g_ref[...]


def _out_proj(merged, w_bf, x2, g, tm=512):
    m, d = x2.shape
    return pl.pallas_call(
        _out_proj_kernel,
        grid=(m // tm,),
        in_specs=[pl.BlockSpec((tm, d), lambda i: (i, 0)),
                  pl.BlockSpec((d, d), lambda i: (0, 0)),
                  pl.BlockSpec((tm, d), lambda i: (i, 0)),
                  pl.BlockSpec((1, d), lambda i: (0, 0))],
        out_specs=pl.BlockSpec((tm, d), lambda i: (i, 0)),
        out_shape=jax.ShapeDtypeStruct((m, d), F32),
        compiler_params=pltpu.CompilerParams(
            dimension_semantics=("parallel",), vmem_limit_bytes=VMEM_LIMIT),
        name="out_proj",
    )(merged, w_bf, x2, g.reshape(1, d))


def _rotary_tables(seq):
    half = HEAD_DIM // 2
    inv_freq = ROPE_BASE ** (-jnp.arange(half, dtype=F32) / half)
    ang = jnp.arange(seq, dtype=F32)[:, None] * inv_freq[None, :]
    cos, sin = jnp.cos(ang), jnp.sin(ang)
    return jnp.concatenate([cos, cos], axis=-1), jnp.concatenate([-sin, sin], axis=-1)


def kernel(x, pre_norm_g, post_norm_g, w_in, conv_w, conv_b, w_branch_a, w_branch_c, w_branch_r, w_out):
    batch, seq, d = x.shape
    assert d == D_MODEL and w_in.shape[-1] == N_IN
    assert seq % (max(DILATIONS) * ATTN_TILE) == 0 and seq % RET_CHUNK == 0
    cos2, sin2 = _rotary_tables(seq)
    log_gamma = jnp.log1p(-jnp.exp2(-5.0 - jnp.arange(N_HEADS, dtype=F32)))
    x2 = x.reshape(batch * seq, d)
    for layer in range(w_in.shape[0]):
        proj = _in_proj(x2, pre_norm_g[layer], w_in[layer].astype(BF16))
        oa = _attention(proj, batch, seq)
        orr = _retention(proj, log_gamma, cos2, sin2, batch, seq)
        merged = _merge(proj, oa, orr, conv_w[layer], conv_b[layer],
                        w_branch_a[layer].astype(BF16), w_branch_c[layer].astype(BF16),
                        w_branch_r[layer].astype(BF16), seq)
        x2 = _out_proj(merged, w_out[layer].astype(BF16), x2, post_norm_g[layer])
    return x2.reshape(batch, seq, d)
```

```python
import functools
import math

import jax
import jax.numpy as jnp
from jax import lax
from jax.experimental import pallas as pl
from jax.experimental.pallas import tpu as pltpu

F32 = jnp.float32
BF16 = jnp.bfloat16

LANE = 128
D_MODEL = 2048
HEAD_DIM = 128
N_HEADS = 12
WIDTH = N_HEADS * HEAD_DIM
N_IN = 12 * WIDTH + 3 * D_MODEL
N_SLABS = N_IN // LANE
DILATIONS = (1, 4, 16)
ATTN_TILE = 128
ATTN_BATCH = 8
LOG2_E = math.log2(math.e)
RET_CHUNK = 128
RET_BATCH = 8
CONV_K = 3
ROPE_BASE = 10000.0
NORM_EPS = 1e-6
NEG_BIG = -1e30

SLAB_QA, SLAB_KA, SLAB_VA, SLAB_ZA = 0, 12, 24, 36
SLAB_UC, SLAB_BC, SLAB_CC, SLAB_ZC = 48, 60, 72, 84
SLAB_QR, SLAB_KR, SLAB_VR, SLAB_ZR = 96, 108, 120, 132
SLAB_GA, SLAB_GC, SLAB_GR = 144, 160, 176

VMEM_LIMIT = 56 * 1024 * 1024


def _sigmoid(x):
    return 0.5 * jnp.tanh(0.5 * x) + 0.5


def _silu(x):
    return x * _sigmoid(x)


def _dot_nt(a, b):
    return lax.dot_general(a, b, (((1,), (1,)), ((), ())), preferred_element_type=F32)


def _dot(a, b):
    return jnp.dot(a, b, preferred_element_type=F32)


def _in_proj_kernel(x_ref, g_ref, w_ref, o_ref, h_ref):
    @pl.when(pl.program_id(1) == 0)
    def _():
        xf = x_ref[...]
        ms = jnp.mean(xf * xf, axis=-1, keepdims=True)
        h_ref[...] = (xf * lax.rsqrt(ms + NORM_EPS) * g_ref[...]).astype(BF16)

    acc = _dot(h_ref[...], w_ref[...].astype(BF16))
    for c in range(o_ref.shape[0]):
        o_ref[c] = acc[:, c * LANE:(c + 1) * LANE].astype(BF16)


def _in_proj(x2, g, w_all, layer, tm=1024, tn=1024):
    m, d = x2.shape
    n = w_all.shape[2]
    return pl.pallas_call(
        _in_proj_kernel,
        grid=(m // tm, n // tn),
        in_specs=[
            pl.BlockSpec((tm, d), lambda i, j: (i, 0)),
            pl.BlockSpec((1, d), lambda i, j: (0, 0)),
            pl.BlockSpec((None, d, tn), lambda i, j: (layer, 0, j)),
        ],
        out_specs=pl.BlockSpec((tn // LANE, tm, LANE), lambda i, j: (j, i, 0)),
        out_shape=jax.ShapeDtypeStruct((n // LANE, m, LANE), BF16),
        scratch_shapes=[pltpu.VMEM((tm, d), BF16)],
        compiler_params=pltpu.CompilerParams(
            dimension_semantics=("parallel", "arbitrary"), vmem_limit_bytes=VMEM_LIMIT),
        name="in_proj",
    )(x2, g.reshape(1, d), w_all)


def _attn_kernel(q_ref, k_ref, v_ref, z_ref, o_ref, qf, kf, vf, a4, m4, l4, a16, m16, l16):
    seq = q_ref.shape[1]
    t = ATTN_TILE
    g = ATTN_BATCH
    e = HEAD_DIM
    pad = kf.shape[0] - seq
    qf[...] = q_ref[0].astype(F32) * (HEAD_DIM ** -0.5 * LOG2_E)
    kf[pl.ds(0, pad), :] = jnp.zeros((pad, e), F32)
    vf[pl.ds(0, pad), :] = jnp.zeros((pad, e), F32)
    kf[pl.ds(pad, seq), :] = k_ref[0].astype(F32)
    vf[pl.ds(pad, seq), :] = v_ref[0].astype(F32)

    row = lax.broadcasted_iota(jnp.int32, (g, t, t), 1)
    col = lax.broadcasted_iota(jnp.int32, (g, t, t), 2)
    slot = lax.broadcasted_iota(jnp.int32, (g, t, t), 0)
    cur_ok = col <= row
    prev_ok = col >= row

    def bdot(a, b, contract_b):
        return lax.dot_general(a, b, (((2,), (contract_b,)), ((0,), (0,))),
                               preferred_element_type=F32)

    def strided(ref, start, n, dil):
        if dil == 1:
            return ref[pl.ds(pl.multiple_of(start, t), n), :]
        return ref[pl.ds(start, n, stride=dil), :]

    def partials(segments, tile_idx, dil):
        qs, kps, kcs, vps, vcs = [], [], [], [], []
        for start, n in segments:
            qs.append(strided(qf, start, n * t, dil).astype(BF16).reshape(n, t, e))
            k_all = strided(kf, pad + start - dil * t, (n + 1) * t, dil).astype(BF16)
            v_all = strided(vf, pad + start - dil * t, (n + 1) * t, dil).astype(BF16)
            v_all = jnp.concatenate([v_all, jnp.ones(v_all.shape, BF16)], axis=1)
            kps.append(k_all[:n * t].reshape(n, t, e))
            kcs.append(k_all[t:].reshape(n, t, e))
            vps.append(v_all[:n * t].reshape(n, t, 2 * e))
            vcs.append(v_all[t:].reshape(n, t, 2 * e))
        cat = lambda xs: xs[0] if len(xs) == 1 else jnp.concatenate(xs, axis=0)
        q = cat(qs)
        s_c = jnp.where(cur_ok, bdot(q, cat(kcs), 2), NEG_BIG)
        s_p = jnp.where(jnp.logical_and(prev_ok, tile_idx > 0), bdot(q, cat(kps), 2), NEG_BIG)
        m = jnp.max(jnp.maximum(s_c, s_p), axis=-1, keepdims=True)
        p_c = jnp.exp2(s_c - m).astype(BF16)
        p_p = jnp.exp2(s_p - m).astype(BF16)
        acc_l = bdot(p_c, cat(vcs), 1) + bdot(p_p, cat(vps), 1)
        return acc_l[:, :, :e], m, acc_l[:, :, e:]

    def dilated_pass(dil, a_s, m_s, l_s):
        tiles_per_class = seq // (dil * t)
        if g <= tiles_per_class:
            batches_per_class = tiles_per_class // g

            def layout(i):
                blk0 = (i % batches_per_class) * g
                start = i // batches_per_class + blk0 * (dil * t)
                return [(start, g)], slot + blk0
        else:
            classes_per_batch = g // tiles_per_class

            def layout(i):
                return ([(i * classes_per_batch + c, tiles_per_class)
                         for c in range(classes_per_batch)], slot % tiles_per_class)

        def body(i, carry):
            segments, tile_idx = layout(i)
            acc, m, l = partials(segments, tile_idx, dil)
            m = jnp.broadcast_to(m, acc.shape)
            done = 0
            for start, n in segments:
                dst = pl.ds(start, n * t, stride=dil)
                a_s[dst, :] = acc[done:done + n].reshape(n * t, e)
                m_s[dst, :] = m[done:done + n].reshape(n * t, e)
                l_s[dst, :] = l[done:done + n].reshape(n * t, e)
                done += n
            return carry

        lax.fori_loop(0, seq // (g * t), body, 0)

    dilated_pass(4, a4, m4, l4)
    dilated_pass(16, a16, m16, l16)

    def final_body(i, carry):
        start = pl.multiple_of(i * (g * t), g * t)
        acc1, m1, l1 = partials([(start, g)], slot + i * g, 1)
        sl = pl.ds(start, g * t)
        load = lambda ref: ref[sl, :].reshape(g, t, e)
        mm4, mm16 = load(m4), load(m16)
        m = jnp.maximum(jnp.maximum(m1, mm4), mm16)
        w1 = jnp.exp2(m1 - m)
        w4 = jnp.exp2(mm4 - m)
        w16 = jnp.exp2(mm16 - m)
        num = w1 * acc1 + w4 * load(a4) + w16 * load(a16)
        den = w1 * l1 + w4 * load(l4) + w16 * load(l16)
        z = z_ref[0, sl, :].astype(F32).reshape(g, t, e)
        o_ref[sl, :] = (num / den * _silu(z)).astype(BF16).reshape(g * t, e)
        return carry

    lax.fori_loop(0, seq // (g * t), final_body, 0)


def _attention(proj, batch, seq):
    m = proj.shape[1]

    def spec(base):
        return pl.BlockSpec((1, seq, LANE), lambda b, h: (base + h, b, 0))

    return pl.pallas_call(
        _attn_kernel,
        grid=(batch, N_HEADS),
        in_specs=[spec(SLAB_QA), spec(SLAB_KA), spec(SLAB_VA), spec(SLAB_ZA)],
        out_specs=pl.BlockSpec((seq, LANE), lambda b, h: (b, h)),
        out_shape=jax.ShapeDtypeStruct((m, WIDTH), BF16),
        scratch_shapes=([pltpu.VMEM((seq, LANE), F32)]
                        + [pltpu.VMEM((seq + max(DILATIONS) * ATTN_TILE, LANE), F32)] * 2
                        + [pltpu.VMEM((seq, LANE), F32)] * 6),
        compiler_params=pltpu.CompilerParams(
            dimension_semantics=("parallel", "parallel"), vmem_limit_bytes=VMEM_LIMIT),
        name="dilated_attn",
    )(proj, proj, proj, proj)


def _ret_kernel(lg_ref, q_ref, k_ref, v_ref, z_ref, cos_ref, sin_ref, o_ref):
    seq = q_ref.shape[1]
    c = RET_CHUNK
    lg = lg_ref[pl.program_id(1)]
    row = lax.broadcasted_iota(jnp.int32, (c, c), 0).astype(F32)
    col = lax.broadcasted_iota(jnp.int32, (c, c), 1).astype(F32)
    rel = row - col
    decay_mask = jnp.where(rel >= 0, jnp.exp(jnp.maximum(rel, 0.0) * lg), 0.0)
    q_decay = jnp.exp((row + 1.0) * lg)
    k_decay = jnp.exp((c - 1.0 - row) * lg)
    chunk_decay = jnp.exp(jnp.full((c, c), float(c), F32) * lg)

    g = RET_BATCH
    e = HEAD_DIM

    def bdot(a, b, contract_a, contract_b):
        return lax.dot_general(a, b, (((contract_a,), (contract_b,)), ((0,), (0,))),
                               preferred_element_type=F32)

    def body(i, state):
        sl = pl.ds(pl.multiple_of(i * (g * c), g * c), g * c)
        cos, sin = cos_ref[sl, :], sin_ref[sl, :]
        rotate = lambda x: x * cos + pltpu.roll(x, e // 2, 1) * sin
        q = rotate(q_ref[0, sl, :].astype(F32)).reshape(g, c, e)
        k = (rotate(k_ref[0, sl, :].astype(F32)) * (HEAD_DIM ** -0.5)).reshape(g, c, e)
        v = v_ref[0, sl, :].reshape(g, c, e)
        qb = q.astype(BF16)
        inner = bdot(qb, k.astype(BF16), 2, 2) * decay_mask
        o = bdot(inner.astype(BF16), v, 2, 1)
        kv = bdot((k * k_decay).astype(BF16), v, 1, 1)
        states = []
        for j in range(g):
            states.append(state)
            state = state * chunk_decay + kv[j]
        o = o + bdot(qb, jnp.stack(states).astype(BF16), 2, 1) * q_decay
        o = o * lax.rsqrt(jnp.mean(o * o, axis=-1, keepdims=True) + NORM_EPS)
        z = z_ref[0, sl, :].astype(F32).reshape(g, c, e)
        o_ref[sl, :] = (o * _silu(z)).astype(BF16).reshape(g * c, e)
        return state

    lax.fori_loop(0, seq // (g * c), body, jnp.zeros((e, e), F32))


def _retention(proj, log_gamma, cos2, sin2, batch, seq):
    m = proj.shape[1]

    def spec(base):
        return pl.BlockSpec((1, seq, LANE), lambda b, h, lg: (base + h, b, 0))

    table = pl.BlockSpec((seq, LANE), lambda b, h, lg: (0, 0))
    return pl.pallas_call(
        _ret_kernel,
        grid_spec=pltpu.PrefetchScalarGridSpec(
            num_scalar_prefetch=1,
            grid=(batch, N_HEADS),
            in_specs=[spec(SLAB_QR), spec(SLAB_KR), spec(SLAB_VR), spec(SLAB_ZR), table, table],
            out_specs=pl.BlockSpec((seq, LANE), lambda b, h, lg: (b, h)),
        ),
        out_shape=jax.ShapeDtypeStruct((m, WIDTH), BF16),
        compiler_params=pltpu.CompilerParams(
            dimension_semantics=("parallel", "parallel"), vmem_limit_bytes=VMEM_LIMIT),
        name="retention",
    )(log_gamma, proj, proj, proj, proj, cos2, sin2)


def _merge_kernel(oa_ref, or_ref, u_ref, b_ref, c_ref, z_ref, uh_ref, ch_ref,
                  ga_ref, gc_ref, gr_ref, cw_ref, cb_ref, wa_ref, wc_ref, wr_ref,
                  o_ref, c_cat, *, tiles_per_seq):
    @pl.when(pl.program_id(1) == 0)
    def _():
        at_seq_start = (pl.program_id(0) % tiles_per_seq) == 0
        row = lax.broadcasted_iota(jnp.int32, (8, LANE), 0)
        for h in range(N_HEADS):
            cu = c_ref[h].astype(F32) * u_ref[h].astype(F32)
            halo = ch_ref[h].astype(F32) * uh_ref[h].astype(F32)
            halo = jnp.where(at_seq_start, 0.0, halo)
            back1 = pltpu.roll(cu, 1, 0)
            back2 = pltpu.roll(cu, 2, 0)
            head1 = jnp.where(row == 0, halo[7:8, :], back1[:8])
            head2 = jnp.where(row == 0, halo[6:7, :], jnp.where(row == 1, halo[7:8, :], back2[:8]))
            back1 = jnp.concatenate([head1, back1[8:]], axis=0)
            back2 = jnp.concatenate([head2, back2[8:]], axis=0)
            conv = (cb_ref[h:h + 1, :] + cw_ref[2, h:h + 1, :] * cu
                    + cw_ref[0, h:h + 1, :] * back2 + cw_ref[1, h:h + 1, :] * back1)
            hz = 0.5 * z_ref[h].astype(F32)
            gated = (conv * (b_ref[h].astype(F32) * hz)) * (1.0 + jnp.tanh(hz))
            c_cat[:, h * LANE:(h + 1) * LANE] = gated.astype(BF16)

    ya = _dot(oa_ref[...], wa_ref[...].astype(BF16))
    yc = _dot(c_cat[...], wc_ref[...].astype(BF16))
    yr = _dot(or_ref[...], wr_ref[...].astype(BF16))
    for s in range(ga_ref.shape[0]):
        lanes = slice(s * LANE, (s + 1) * LANE)
        merged = (_sigmoid(ga_ref[s].astype(F32)) * ya[:, lanes]
                  + _sigmoid(gc_ref[s].astype(F32)) * yc[:, lanes]
                  + _sigmoid(gr_ref[s].astype(F32)) * yr[:, lanes])
        o_ref[:, lanes] = merged.astype(BF16)


def _merge(proj, oa, orr, conv_w, conv_b, wa, wc, wr, layer, seq, tm=512, tn=512):
    m = proj.shape[1]
    gs = tn // LANE
    halo_rows = 8
    hb = tm // halo_rows

    def branch(base):
        return pl.BlockSpec((N_HEADS, tm, LANE), lambda i, n: (base // N_HEADS, i, 0))

    def halo(base):
        return pl.BlockSpec((N_HEADS, halo_rows, LANE),
                            lambda i, n: (base // N_HEADS, jnp.maximum(i * hb - 1, 0), 0))

    def gate(base):
        return pl.BlockSpec((gs, tm, LANE), lambda i, n: (base // gs + n, i, 0))

    wspec = pl.BlockSpec((None, WIDTH, tn), lambda i, n: (layer, 0, n))
    own = pl.BlockSpec((tm, WIDTH), lambda i, n: (i, 0))
    return pl.pallas_call(
        functools.partial(_merge_kernel, tiles_per_seq=seq // tm),
        grid=(m // tm, D_MODEL // tn),
        in_specs=[own, own, branch(SLAB_UC), branch(SLAB_BC), branch(SLAB_CC), branch(SLAB_ZC),
                  halo(SLAB_UC), halo(SLAB_CC), gate(SLAB_GA), gate(SLAB_GC), gate(SLAB_GR),
                  pl.BlockSpec((CONV_K, N_HEADS, LANE), lambda i, n: (0, 0, 0)),
                  pl.BlockSpec((N_HEADS, LANE), lambda i, n: (0, 0)),
                  wspec, wspec, wspec],
        out_specs=pl.BlockSpec((tm, tn), lambda i, n: (i, n)),
        out_shape=jax.ShapeDtypeStruct((m, D_MODEL), BF16),
        scratch_shapes=[pltpu.VMEM((tm, WIDTH), BF16)],
        compiler_params=pltpu.CompilerParams(
            dimension_semantics=("parallel", "arbitrary"), vmem_limit_bytes=VMEM_LIMIT),
        name="branch_merge",
    )(oa, orr, proj, proj, proj, proj, proj, proj, proj, proj, proj,
      conv_w.reshape(CONV_K, N_HEADS, LANE), conv_b.reshape(N_HEADS, LANE), wa, wc, wr)


def _out_proj_kernel(m_ref, w_ref, x_ref, g_ref, o_ref):
    y = _dot(m_ref[...], w_ref[...].astype(BF16))
    ms = jnp.mean(y * y, axis=-1, keepdims=True)
    o_ref[...] = x_ref[...] + y * lax.rsqrt(ms + NORM_EPS) * g_ref[...]


def _out_proj(merged, w_all, layer, x2, g, tm=512):
    m, d = x2.shape
    return pl.pallas_call(
        _out_proj_kernel,
        grid=(m // tm,),
        in_specs=[pl.BlockSpec((tm, d), lambda i: (i, 0)),
                  pl.BlockSpec((None, d, d), lambda i: (layer, 0, 0), pipeline_mode=pl.Buffered(1)),
                  pl.BlockSpec((tm, d), lambda i: (i, 0)),
                  pl.BlockSpec((1, d), lambda i: (0, 0))],
        out_specs=pl.BlockSpec((tm, d), lambda i: (i, 0)),
        out_shape=jax.ShapeDtypeStruct((m, d), F32),
        compiler_params=pltpu.CompilerParams(
            dimension_semantics=("parallel",), vmem_limit_bytes=VMEM_LIMIT),
        name="out_proj",
    )(merged, w_all, x2, g.reshape(1, d))


def _rotary_tables(seq):
    half = HEAD_DIM // 2
    inv_freq = ROPE_BASE ** (-jnp.arange(half, dtype=F32) / half)
    ang = jnp.arange(seq, dtype=F32)[:, None] * inv_freq[None, :]
    cos, sin = jnp.cos(ang), jnp.sin(ang)
    return jnp.concatenate([cos, cos], axis=-1), jnp.concatenate([-sin, sin], axis=-1)


def kernel(x, pre_norm_g, post_norm_g, w_in, conv_w, conv_b, w_branch_a, w_branch_c, w_branch_r, w_out):
    batch, seq, d = x.shape
    assert d == D_MODEL and w_in.shape[-1] == N_IN
    assert seq % (max(DILATIONS) * ATTN_TILE) == 0 and seq % RET_CHUNK == 0
    cos2, sin2 = _rotary_tables(seq)
    log_gamma = jnp.log1p(-jnp.exp2(-5.0 - jnp.arange(N_HEADS, dtype=F32)))
    x2 = x.reshape(batch * seq, d)
    for layer in range(w_in.shape[0]):
        proj = _in_proj(x2, pre_norm_g[layer], w_in, layer)
        oa = _attention(proj, batch, seq)
        orr = _retention(proj, log_gamma, cos2, sin2, batch, seq)
        merged = _merge(proj, oa, orr, conv_w[layer], conv_b[layer],
                        w_branch_a, w_branch_c, w_branch_r, layer, seq)
        x2 = _out_proj(merged, w_out, layer, x2, post_norm_g[layer])
    return x2.reshape(batch, seq, d)
```

```python
import functools
import math

import jax
import jax.numpy as jnp
from jax import lax
from jax.experimental import pallas as pl
from jax.experimental.pallas import tpu as pltpu

F32 = jnp.float32
BF16 = jnp.bfloat16

LANE = 128
D_MODEL = 2048
HEAD_DIM = 128
N_HEADS = 12
WIDTH = N_HEADS * HEAD_DIM
N_IN = 12 * WIDTH + 3 * D_MODEL
DILATIONS = (1, 4, 16)
ATTN_TILE = 128
ATTN_BATCH = {1: 16, 4: 16, 16: 8}
LOG2_E = math.log2(math.e)
RET_CHUNK = 128
RET_BATCH = 8
CONV_K = 3
ROPE_BASE = 10000.0
NORM_EPS = 1e-6
NEG_BIG = -1e30

COL_QA, COL_KA, COL_VA, COL_ZA = 0, 12, 24, 36
COL_UC, COL_BC, COL_CC, COL_ZC = 48, 60, 72, 84
COL_QR, COL_KR, COL_VR, COL_ZR = 96, 108, 120, 132
COL_GA, COL_GC, COL_GR = 144, 160, 176
CONV_SLABS = COL_QR - COL_UC
SLAB_QA, SLAB_KA, SLAB_VA, SLAB_ZA = COL_QA, COL_KA, COL_VA, COL_ZA
SLAB_QR, SLAB_KR, SLAB_VR, SLAB_ZR = (c - CONV_SLABS for c in (COL_QR, COL_KR, COL_VR, COL_ZR))
SLAB_GA, SLAB_GC, SLAB_GR = (c - CONV_SLABS for c in (COL_GA, COL_GC, COL_GR))

VMEM_LIMIT = 56 * 1024 * 1024


def _sigmoid(x):
    return 0.5 * jnp.tanh(0.5 * x) + 0.5


def _silu(x):
    return x * _sigmoid(x)


def _dot(a, b):
    return jnp.dot(a, b, preferred_element_type=F32)


def _rms_norm_bf16(x_ref, g_ref):
    xf = x_ref[...]
    ms = jnp.mean(xf * xf, axis=-1, keepdims=True)
    return (xf * lax.rsqrt(ms + NORM_EPS) * g_ref[...]).astype(BF16)


def _in_proj_kernel(x_ref, g_ref, w_ref, o_ref, h_ref):
    @pl.when(pl.program_id(1) == 0)
    def _():
        h_ref[...] = _rms_norm_bf16(x_ref, g_ref)

    acc = _dot(h_ref[...], w_ref[...].astype(BF16))
    for c in range(o_ref.shape[0]):
        o_ref[c] = acc[:, c * LANE:(c + 1) * LANE].astype(BF16)


def _in_proj(x2, g, w_all, layer, tm=1024, tn=1024):
    m, d = x2.shape
    first_conv = COL_UC * LANE // tn
    n_conv = CONV_SLABS * LANE // tn
    n = w_all.shape[2] - CONV_SLABS * LANE
    return pl.pallas_call(
        _in_proj_kernel,
        grid=(m // tm, n // tn),
        in_specs=[
            pl.BlockSpec((tm, d), lambda i, j: (i, 0)),
            pl.BlockSpec((1, d), lambda i, j: (0, 0)),
            pl.BlockSpec((None, d, tn),
                         lambda i, j: (layer, 0, jnp.where(j < first_conv, j, j + n_conv))),
        ],
        out_specs=pl.BlockSpec((tn // LANE, tm, LANE), lambda i, j: (j, i, 0)),
        out_shape=jax.ShapeDtypeStruct((n // LANE, m, LANE), BF16),
        scratch_shapes=[pltpu.VMEM((tm, d), BF16)],
        compiler_params=pltpu.CompilerParams(
            dimension_semantics=("parallel", "arbitrary"), vmem_limit_bytes=VMEM_LIMIT),
        name="in_proj",
    )(x2, g.reshape(1, d), w_all)


def _conv_proj_kernel(x_ref, g_ref, wu_ref, wb_ref, wc_ref, wz_ref, cw_ref, cb_ref, o_ref,
                      h_ref, halo_ref, *, tiles_per_seq):
    i = pl.program_id(0)
    jc = pl.program_id(1)
    tm = x_ref.shape[0]

    @pl.when(jc == 0)
    def _():
        h_ref[...] = _rms_norm_bf16(x_ref, g_ref)

    @pl.when(i == 0)
    def _():
        halo_ref[jc] = jnp.zeros(halo_ref.shape[1:], F32)

    h = h_ref[...]
    cu = _dot(h, wc_ref[...].astype(BF16)) * _dot(h, wu_ref[...].astype(BF16))
    halo = jnp.where((i % tiles_per_seq) == 0, 0.0, halo_ref[jc])
    halo_ref[jc] = cu[tm - 8:, :]
    row = lax.broadcasted_iota(jnp.int32, (8, cu.shape[1]), 0)
    back1 = pltpu.roll(cu, 1, 0)
    back2 = pltpu.roll(cu, 2, 0)
    head1 = jnp.where(row == 0, halo[7:8, :], back1[:8])
    head2 = jnp.where(row == 0, halo[6:7, :], jnp.where(row == 1, halo[7:8, :], back2[:8]))
    back1 = jnp.concatenate([head1, back1[8:]], axis=0)
    back2 = jnp.concatenate([head2, back2[8:]], axis=0)
    conv = (cb_ref[...] + cw_ref[2:3, :] * cu + cw_ref[0:1, :] * back2 + cw_ref[1:2, :] * back1)
    hz = 0.5 * _dot(h, wz_ref[...].astype(BF16))
    gated = (conv * (_dot(h, wb_ref[...].astype(BF16)) * hz)) * (1.0 + jnp.tanh(hz))
    o_ref[...] = gated.astype(BF16)


def _conv_proj(x2, g, w_all, conv_w, conv_b, layer, seq, tm=1024, tc=256):
    m, d = x2.shape
    n_groups = WIDTH // tc

    def wspec(col):
        return pl.BlockSpec((None, d, tc), lambda i, jc: (layer, 0, col * LANE // tc + jc))

    return pl.pallas_call(
        functools.partial(_conv_proj_kernel, tiles_per_seq=seq // tm),
        grid=(m // tm, n_groups),
        in_specs=[pl.BlockSpec((tm, d), lambda i, jc: (i, 0)),
                  pl.BlockSpec((1, d), lambda i, jc: (0, 0)),
                  wspec(COL_UC), wspec(COL_BC), wspec(COL_CC), wspec(COL_ZC),
                  pl.BlockSpec((CONV_K, tc), lambda i, jc: (0, jc)),
                  pl.BlockSpec((1, tc), lambda i, jc: (0, jc))],
        out_specs=pl.BlockSpec((tm, tc), lambda i, jc: (i, jc)),
        out_shape=jax.ShapeDtypeStruct((m, WIDTH), BF16),
        scratch_shapes=[pltpu.VMEM((tm, d), BF16), pltpu.VMEM((n_groups, 8, tc), F32)],
        compiler_params=pltpu.CompilerParams(
            dimension_semantics=("arbitrary", "arbitrary"), vmem_limit_bytes=VMEM_LIMIT),
        name="conv_proj",
    )(x2, g.reshape(1, d), w_all, w_all, w_all, w_all, conv_w, conv_b.reshape(1, WIDTH))


def _attn_kernel(q_ref, k_ref, v_ref, z_ref, o_ref, qf, kf, vf, a4, m4, l4, a16, m16, l16):
    seq = q_ref.shape[1]
    t = ATTN_TILE
    e = HEAD_DIM
    pad = kf.shape[0] - seq
    qf[...] = q_ref[0].astype(F32) * (HEAD_DIM ** -0.5 * LOG2_E)
    kf[pl.ds(0, pad), :] = jnp.zeros((pad, e), F32)
    vf[pl.ds(0, pad), :] = jnp.zeros((pad, e), F32)
    kf[pl.ds(pad, seq), :] = k_ref[0].astype(F32)
    vf[pl.ds(pad, seq), :] = v_ref[0].astype(F32)

    row = lax.broadcasted_iota(jnp.int32, (t, t), 0)
    col = lax.broadcasted_iota(jnp.int32, (t, t), 1)
    cur_ok = col <= row
    prev_ok = col >= row

    def bdot(a, b, contract_b):
        return lax.dot_general(a, b, (((2,), (contract_b,)), ((0,), (0,))),
                               preferred_element_type=F32)

    def strided(ref, start, n, dil):
        if dil == 1:
            return ref[pl.ds(pl.multiple_of(start, t), n), :]
        return ref[pl.ds(start, n, stride=dil), :]

    def partials(segments, tile_idx, dil):
        qs, kps, kcs, vps, vcs = [], [], [], [], []
        for start, n in segments:
            qs.append(strided(qf, start, n * t, dil).astype(BF16).reshape(n, t, e))
            k_all = strided(kf, pad + start - dil * t, (n + 1) * t, dil).astype(BF16)
            v_all = strided(vf, pad + start - dil * t, (n + 1) * t, dil).astype(BF16)
            v_all = jnp.concatenate([v_all, jnp.ones(v_all.shape, BF16)], axis=1)
            kps.append(k_all[:n * t].reshape(n, t, e))
            kcs.append(k_all[t:].reshape(n, t, e))
            vps.append(v_all[:n * t].reshape(n, t, 2 * e))
            vcs.append(v_all[t:].reshape(n, t, 2 * e))
        cat = lambda xs: xs[0] if len(xs) == 1 else jnp.concatenate(xs, axis=0)
        q = cat(qs)
        s_c = jnp.where(cur_ok, bdot(q, cat(kcs), 2), NEG_BIG)
        s_p = jnp.where(jnp.logical_and(prev_ok, tile_idx > 0), bdot(q, cat(kps), 2), NEG_BIG)
        m = jnp.max(jnp.maximum(s_c, s_p), axis=-1, keepdims=True)
        p_c = jnp.exp2(s_c - m).astype(BF16)
        p_p = jnp.exp2(s_p - m).astype(BF16)
        acc_l = bdot(p_c, cat(vcs), 1) + bdot(p_p, cat(vps), 1)
        return acc_l[:, :, :e], m, acc_l[:, :, e:]

    def dilated_pass(dil, a_s, m_s, l_s):
        g = ATTN_BATCH[dil]
        slot = lax.broadcasted_iota(jnp.int32, (g, t, t), 0)
        tiles_per_class = seq // (dil * t)
        if g <= tiles_per_class:
            batches_per_class = tiles_per_class // g

            def layout(i):
                blk0 = (i % batches_per_class) * g
                start = i // batches_per_class + blk0 * (dil * t)
                return [(start, g)], slot + blk0
        else:
            classes_per_batch = g // tiles_per_class

            def layout(i):
                return ([(i * classes_per_batch + c, tiles_per_class)
                         for c in range(classes_per_batch)], slot % tiles_per_class)

        def body(i, carry):
            segments, tile_idx = layout(i)
            acc, m, l = partials(segments, tile_idx, dil)
            m = jnp.broadcast_to(m, acc.shape)
            done = 0
            for start, n in segments:
                dst = pl.ds(start, n * t, stride=dil)
                a_s[dst, :] = acc[done:done + n].reshape(n * t, e)
                m_s[dst, :] = m[done:done + n].reshape(n * t, e)
                l_s[dst, :] = l[done:done + n].reshape(n * t, e)
                done += n
            return carry

        lax.fori_loop(0, seq // (g * t), body, 0)

    dilated_pass(4, a4, m4, l4)
    dilated_pass(16, a16, m16, l16)

    g = ATTN_BATCH[1]
    slot = lax.broadcasted_iota(jnp.int32, (g, t, t), 0)

    def final_body(i, carry):
        start = pl.multiple_of(i * (g * t), g * t)
        acc1, m1, l1 = partials([(start, g)], slot + i * g, 1)
        sl = pl.ds(start, g * t)
        load = lambda ref: ref[sl, :].reshape(g, t, e)
        mm4, mm16 = load(m4), load(m16)
        m = jnp.maximum(jnp.maximum(m1, mm4), mm16)
        w1 = jnp.exp2(m1 - m)
        w4 = jnp.exp2(mm4 - m)
        w16 = jnp.exp2(mm16 - m)
        num = w1 * acc1 + w4 * load(a4) + w16 * load(a16)
        den = w1 * l1 + w4 * load(l4) + w16 * load(l16)
        z = z_ref[0, sl, :].astype(F32).reshape(g, t, e)
        o_ref[sl, :] = (num / den * _silu(z)).astype(BF16).reshape(g * t, e)
        return carry

    lax.fori_loop(0, seq // (g * t), final_body, 0)


def _attention(proj, batch, seq):
    m = proj.shape[1]

    def spec(base):
        return pl.BlockSpec((1, seq, LANE), lambda b, h: (base + h, b, 0))

    return pl.pallas_call(
        _attn_kernel,
        grid=(batch, N_HEADS),
        in_specs=[spec(SLAB_QA), spec(SLAB_KA), spec(SLAB_VA), spec(SLAB_ZA)],
        out_specs=pl.BlockSpec((seq, LANE), lambda b, h: (b, h)),
        out_shape=jax.ShapeDtypeStruct((m, WIDTH), BF16),
        scratch_shapes=([pltpu.VMEM((seq, LANE), F32)]
                        + [pltpu.VMEM((seq + max(DILATIONS) * ATTN_TILE, LANE), F32)] * 2
                        + [pltpu.VMEM((seq, LANE), F32)] * 6),
        compiler_params=pltpu.CompilerParams(
            dimension_semantics=("parallel", "parallel"), vmem_limit_bytes=VMEM_LIMIT),
        name="dilated_attn",
    )(proj, proj, proj, proj)


def _ret_kernel(lg_ref, q_ref, k_ref, v_ref, z_ref, cos_ref, sin_ref, o_ref):
    seq = q_ref.shape[1]
    c = RET_CHUNK
    lg = lg_ref[pl.program_id(1)]
    row = lax.broadcasted_iota(jnp.int32, (c, c), 0).astype(F32)
    col = lax.broadcasted_iota(jnp.int32, (c, c), 1).astype(F32)
    rel = row - col
    decay_mask = jnp.where(rel >= 0, jnp.exp(jnp.maximum(rel, 0.0) * lg), 0.0)
    q_decay = jnp.exp((row + 1.0) * lg)
    k_decay = jnp.exp((c - 1.0 - row) * lg)
    chunk_decay = jnp.exp(jnp.full((c, c), float(c), F32) * lg)

    g = RET_BATCH
    e = HEAD_DIM

    def bdot(a, b, contract_a, contract_b):
        return lax.dot_general(a, b, (((contract_a,), (contract_b,)), ((0,), (0,))),
                               preferred_element_type=F32)

    def body(i, state):
        sl = pl.ds(pl.multiple_of(i * (g * c), g * c), g * c)
        cos, sin = cos_ref[sl, :], sin_ref[sl, :]
        rotate = lambda x: x * cos + pltpu.roll(x, e // 2, 1) * sin
        q = rotate(q_ref[0, sl, :].astype(F32)).reshape(g, c, e)
        k = (rotate(k_ref[0, sl, :].astype(F32)) * (HEAD_DIM ** -0.5)).reshape(g, c, e)
        v = v_ref[0, sl, :].reshape(g, c, e)
        qb = q.astype(BF16)
        inner = bdot(qb, k.astype(BF16), 2, 2) * decay_mask
        o = bdot(inner.astype(BF16), v, 2, 1)
        kv = bdot((k * k_decay).astype(BF16), v, 1, 1)
        states = []
        for j in range(g):
            states.append(state)
            state = state * chunk_decay + kv[j]
        o = o + bdot(qb, jnp.stack(states).astype(BF16), 2, 1) * q_decay
        o = o * lax.rsqrt(jnp.mean(o * o, axis=-1, keepdims=True) + NORM_EPS)
        z = z_ref[0, sl, :].astype(F32).reshape(g, c, e)
        o_ref[sl, :] = (o * _silu(z)).astype(BF16).reshape(g * c, e)
        return state

    lax.fori_loop(0, seq // (g * c), body, jnp.zeros((e, e), F32))


def _retention(proj, log_gamma, cos2, sin2, batch, seq):
    m = proj.shape[1]

    def spec(base):
        return pl.BlockSpec((1, seq, LANE), lambda b, h, lg: (base + h, b, 0))

    table = pl.BlockSpec((seq, LANE), lambda b, h, lg: (0, 0))
    return pl.pallas_call(
        _ret_kernel,
        grid_spec=pltpu.PrefetchScalarGridSpec(
            num_scalar_prefetch=1,
            grid=(batch, N_HEADS),
            in_specs=[spec(SLAB_QR), spec(SLAB_KR), spec(SLAB_VR), spec(SLAB_ZR), table, table],
            out_specs=pl.BlockSpec((seq, LANE), lambda b, h, lg: (b, h)),
        ),
        out_shape=jax.ShapeDtypeStruct((m, WIDTH), BF16),
        compiler_params=pltpu.CompilerParams(
            dimension_semantics=("parallel", "parallel"), vmem_limit_bytes=VMEM_LIMIT),
        name="retention",
    )(log_gamma, proj, proj, proj, proj, cos2, sin2)


def _merge_kernel(oa_ref, oc_ref, or_ref, ga_ref, gc_ref, gr_ref, wa_ref, wc_ref, wr_ref, o_ref):
    ya = _dot(oa_ref[...], wa_ref[...].astype(BF16))
    yc = _dot(oc_ref[...], wc_ref[...].astype(BF16))
    yr = _dot(or_ref[...], wr_ref[...].astype(BF16))
    for s in range(ga_ref.shape[0]):
        lanes = slice(s * LANE, (s + 1) * LANE)
        merged = (_sigmoid(ga_ref[s].astype(F32)) * ya[:, lanes]
                  + _sigmoid(gc_ref[s].astype(F32)) * yc[:, lanes]
                  + _sigmoid(gr_ref[s].astype(F32)) * yr[:, lanes])
        o_ref[:, lanes] = merged.astype(BF16)


def _merge(proj, oa, oc, orr, wa, wc, wr, layer, tm=1024, tn=512):
    m = proj.shape[1]
    gs = tn // LANE

    def gate(base):
        return pl.BlockSpec((gs, tm, LANE), lambda i, n: (base // gs + n, i, 0))

    wspec = pl.BlockSpec((None, WIDTH, tn), lambda i, n: (layer, 0, n))
    own = pl.BlockSpec((tm, WIDTH), lambda i, n: (i, 0))
    return pl.pallas_call(
        _merge_kernel,
        grid=(m // tm, D_MODEL // tn),
        in_specs=[own, own, own, gate(SLAB_GA), gate(SLAB_GC), gate(SLAB_GR), wspec, wspec, wspec],
        out_specs=pl.BlockSpec((tm, tn), lambda i, n: (i, n)),
        out_shape=jax.ShapeDtypeStruct((m, D_MODEL), BF16),
        compiler_params=pltpu.CompilerParams(
            dimension_semantics=("parallel", "parallel"), vmem_limit_bytes=VMEM_LIMIT),
        name="branch_merge",
    )(oa, oc, orr, proj, proj, proj, wa, wc, wr)


def _out_proj_kernel(m_ref, w_ref, x_ref, g_ref, o_ref):
    y = _dot(m_ref[...], w_ref[...].astype(BF16))
    ms = jnp.mean(y * y, axis=-1, keepdims=True)
    o_ref[...] = x_ref[...] + y * lax.rsqrt(ms + NORM_EPS) * g_ref[...]


def _out_proj(merged, w_all, layer, x2, g, tm=512):
    m, d = x2.shape
    return pl.pallas_call(
        _out_proj_kernel,
        grid=(m // tm,),
        in_specs=[pl.BlockSpec((tm, d), lambda i: (i, 0)),
                  pl.BlockSpec((None, d, d), lambda i: (layer, 0, 0), pipeline_mode=pl.Buffered(1)),
                  pl.BlockSpec((tm, d), lambda i: (i, 0)),
                  pl.BlockSpec((1, d), lambda i: (0, 0))],
        out_specs=pl.BlockSpec((tm, d), lambda i: (i, 0)),
        out_shape=jax.ShapeDtypeStruct((m, d), F32),
        compiler_params=pltpu.CompilerParams(
            dimension_semantics=("parallel",), vmem_limit_bytes=VMEM_LIMIT),
        name="out_proj",
    )(merged, w_all, x2, g.reshape(1, d))


def _rotary_tables(seq):
    half = HEAD_DIM // 2
    inv_freq = ROPE_BASE ** (-jnp.arange(half, dtype=F32) / half)
    ang = jnp.arange(seq, dtype=F32)[:, None] * inv_freq[None, :]
    cos, sin = jnp.cos(ang), jnp.sin(ang)
    return jnp.concatenate([cos, cos], axis=-1), jnp.concatenate([-sin, sin], axis=-1)


def kernel(x, pre_norm_g, post_norm_g, w_in, conv_w, conv_b, w_branch_a, w_branch_c, w_branch_r, w_out):
    batch, seq, d = x.shape
    assert d == D_MODEL and w_in.shape[-1] == N_IN
    assert seq % (max(DILATIONS) * ATTN_TILE) == 0 and seq % RET_CHUNK == 0
    cos2, sin2 = _rotary_tables(seq)
    log_gamma = jnp.log1p(-jnp.exp2(-5.0 - jnp.arange(N_HEADS, dtype=F32)))
    x2 = x.reshape(batch * seq, d)
    for layer in range(w_in.shape[0]):
        proj = _in_proj(x2, pre_norm_g[layer], w_in, layer)
        oa = _attention(proj, batch, seq)
        orr = _retention(proj, log_gamma, cos2, sin2, batch, seq)
        oc = _conv_proj(x2, pre_norm_g[layer], w_in, conv_w[layer], conv_b[layer], layer, seq)
        merged = _merge(proj, oa, oc, orr, w_branch_a, w_branch_c, w_branch_r, layer)
        x2 = _out_proj(merged, w_out, layer, x2, post_norm_g[layer])
    return x2.reshape(batch, seq, d)
```

```python
import functools
import math

import jax
import jax.numpy as jnp
from jax import lax
from jax.experimental import pallas as pl
from jax.experimental.pallas import tpu as pltpu

F32 = jnp.float32
BF16 = jnp.bfloat16

LANE = 128
D_MODEL = 2048
HEAD_DIM = 128
N_HEADS = 12
WIDTH = N_HEADS * HEAD_DIM
N_IN = 12 * WIDTH + 3 * D_MODEL
DILATIONS = (1, 4, 16)
ATTN_TILE = 128
ATTN_BATCH = {1: 16, 4: 16, 16: 16}
LOG2_E = math.log2(math.e)
RET_CHUNK = 128
RET_BATCH = 8
CONV_K = 3
ROPE_BASE = 10000.0
NORM_EPS = 1e-6
NEG_BIG = -1e30

COL_QA, COL_KA, COL_VA, COL_ZA = 0, 12, 24, 36
COL_UC, COL_BC, COL_CC, COL_ZC = 48, 60, 72, 84
COL_QR, COL_KR, COL_VR, COL_ZR = 96, 108, 120, 132
COL_GA, COL_GC, COL_GR = 144, 160, 176
ATTN_SLABS = COL_UC - COL_QA
SLAB_QA, SLAB_KA, SLAB_VA, SLAB_ZA = COL_QA, COL_KA, COL_VA, COL_ZA
SLAB_QR, SLAB_KR, SLAB_VR, SLAB_ZR = (c - COL_QR for c in (COL_QR, COL_KR, COL_VR, COL_ZR))
SLAB_GA, SLAB_GC, SLAB_GR = (c - COL_QR for c in (COL_GA, COL_GC, COL_GR))
N_CLASSES = 4

VMEM_LIMIT = 56 * 1024 * 1024


def _sigmoid(x):
    return 0.5 * jnp.tanh(0.5 * x) + 0.5


def _silu(x):
    return x * _sigmoid(x)


def _dot(a, b):
    return jnp.dot(a, b, preferred_element_type=F32)


def _rms_norm_bf16(x_ref, g_ref):
    xf = x_ref[...]
    ms = jnp.mean(xf * xf, axis=-1, keepdims=True)
    return (xf * lax.rsqrt(ms + NORM_EPS) * g_ref[...]).astype(BF16)


def _in_proj_kernel(x_ref, g_ref, w_ref, o_ref, h_ref):
    @pl.when(pl.program_id(1) == 0)
    def _():
        h_ref[...] = _rms_norm_bf16(x_ref, g_ref)

    acc = _dot(h_ref[...], w_ref[...].astype(BF16))
    for c in range(o_ref.shape[0]):
        o_ref[c] = acc[:, c * LANE:(c + 1) * LANE].astype(BF16)


def _in_proj(x2, g, w_all, layer, col0, n_slabs, tm=1024, tn=1024):
    m, d = x2.shape
    tile0 = col0 * LANE // tn
    return pl.pallas_call(
        _in_proj_kernel,
        grid=(m // tm, n_slabs * LANE // tn),
        in_specs=[
            pl.BlockSpec((tm, d), lambda i, j: (i, 0)),
            pl.BlockSpec((1, d), lambda i, j: (0, 0)),
            pl.BlockSpec((None, d, tn), lambda i, j: (layer, 0, tile0 + j)),
        ],
        out_specs=pl.BlockSpec((tn // LANE, tm, LANE), lambda i, j: (j, i, 0)),
        out_shape=jax.ShapeDtypeStruct((n_slabs, m, LANE), BF16),
        scratch_shapes=[pltpu.VMEM((tm, d), BF16)],
        compiler_params=pltpu.CompilerParams(
            dimension_semantics=("parallel", "arbitrary"), vmem_limit_bytes=VMEM_LIMIT),
        name="in_proj",
    )(x2, g.reshape(1, d), w_all)


def _attn_proj_kernel(x_ref, g_ref, w_ref, nat_ref, cm_ref, h_ref, scr_ref):
    j = pl.program_id(1)

    @pl.when(j == 0)
    def _():
        h_ref[...] = _rms_norm_bf16(x_ref, g_ref)

    n_slabs, n_cls, rows = cm_ref.shape[0], cm_ref.shape[1], cm_ref.shape[2]
    pair = 2 * LANE
    for c0 in range(0, n_slabs, 2):
        acc = _dot(h_ref[...], w_ref[:, c0 * LANE:c0 * LANE + pair].astype(BF16))
        for c in (c0, c0 + 1):
            is_q = (j * n_slabs + c) < N_HEADS
            slab = (acc[:, (c - c0) * LANE:(c - c0 + 1) * LANE]
                    * jnp.where(is_q, HEAD_DIM ** -0.5 * LOG2_E, 1.0))
            nat_ref[c] = slab.astype(BF16)
            scr_ref[c] = slab
            for r in range(n_cls):
                cm_ref[c, r] = scr_ref[c, pl.ds(r, rows, stride=n_cls), :]


def _attn_proj(x2, g, w_all, layer, batch, seq, tm=1024, tn=512):
    m, d = x2.shape
    tiles_per_seq = seq // tm
    slabs = tn // LANE
    return pl.pallas_call(
        _attn_proj_kernel,
        grid=(m // tm, ATTN_SLABS * LANE // tn),
        in_specs=[
            pl.BlockSpec((tm, d), lambda i, j: (i, 0)),
            pl.BlockSpec((1, d), lambda i, j: (0, 0)),
            pl.BlockSpec((None, d, tn), lambda i, j: (layer, 0, COL_QA * LANE // tn + j)),
        ],
        out_specs=[
            pl.BlockSpec((slabs, tm, LANE), lambda i, j: (j, i, 0)),
            pl.BlockSpec((slabs, None, N_CLASSES, tm // N_CLASSES, LANE),
                         lambda i, j: (j, i // tiles_per_seq, 0, i % tiles_per_seq, 0)),
        ],
        out_shape=[jax.ShapeDtypeStruct((ATTN_SLABS, m, LANE), BF16),
                   jax.ShapeDtypeStruct((ATTN_SLABS, batch, N_CLASSES, seq // N_CLASSES, LANE), F32)],
        scratch_shapes=[pltpu.VMEM((tm, d), BF16), pltpu.VMEM((slabs, tm, LANE), F32)],
        compiler_params=pltpu.CompilerParams(
            dimension_semantics=("parallel", "arbitrary"), vmem_limit_bytes=VMEM_LIMIT),
        name="attn_proj",
    )(x2, g.reshape(1, d), w_all)


def _conv_proj_kernel(x_ref, g_ref, wu_ref, wb_ref, wc_ref, wz_ref, cw_ref, cb_ref, o_ref,
                      h_ref, halo_ref, *, tiles_per_seq):
    i = pl.program_id(0)
    jc = pl.program_id(1)
    tm = x_ref.shape[0]

    @pl.when(jc == 0)
    def _():
        h_ref[...] = _rms_norm_bf16(x_ref, g_ref)

    @pl.when(i == 0)
    def _():
        halo_ref[jc] = jnp.zeros(halo_ref.shape[1:], F32)

    h = h_ref[...]
    cu = _dot(h, wc_ref[...].astype(BF16)) * _dot(h, wu_ref[...].astype(BF16))
    halo = jnp.where((i % tiles_per_seq) == 0, 0.0, halo_ref[jc])
    halo_ref[jc] = cu[tm - 8:, :]
    row = lax.broadcasted_iota(jnp.int32, (8, cu.shape[1]), 0)
    back1 = pltpu.roll(cu, 1, 0)
    back2 = pltpu.roll(cu, 2, 0)
    head1 = jnp.where(row == 0, halo[7:8, :], back1[:8])
    head2 = jnp.where(row == 0, halo[6:7, :], jnp.where(row == 1, halo[7:8, :], back2[:8]))
    back1 = jnp.concatenate([head1, back1[8:]], axis=0)
    back2 = jnp.concatenate([head2, back2[8:]], axis=0)
    conv = (cb_ref[...] + cw_ref[2:3, :] * cu + cw_ref[0:1, :] * back2 + cw_ref[1:2, :] * back1)
    hz = 0.5 * _dot(h, wz_ref[...].astype(BF16))
    gated = (conv * (_dot(h, wb_ref[...].astype(BF16)) * hz)) * (1.0 + jnp.tanh(hz))
    o_ref[...] = gated.astype(BF16)


def _conv_proj(x2, g, w_all, conv_w, conv_b, layer, seq, tm=1024, tc=256):
    m, d = x2.shape
    n_groups = WIDTH // tc

    def wspec(col):
        return pl.BlockSpec((None, d, tc), lambda i, jc: (layer, 0, col * LANE // tc + jc))

    return pl.pallas_call(
        functools.partial(_conv_proj_kernel, tiles_per_seq=seq // tm),
        grid=(m // tm, n_groups),
        in_specs=[pl.BlockSpec((tm, d), lambda i, jc: (i, 0)),
                  pl.BlockSpec((1, d), lambda i, jc: (0, 0)),
                  wspec(COL_UC), wspec(COL_BC), wspec(COL_CC), wspec(COL_ZC),
                  pl.BlockSpec((CONV_K, tc), lambda i, jc: (0, jc)),
                  pl.BlockSpec((1, tc), lambda i, jc: (0, jc))],
        out_specs=pl.BlockSpec((tm, tc), lambda i, jc: (i, jc)),
        out_shape=jax.ShapeDtypeStruct((m, WIDTH), BF16),
        scratch_shapes=[pltpu.VMEM((tm, d), BF16), pltpu.VMEM((n_groups, 8, tc), F32)],
        compiler_params=pltpu.CompilerParams(
            dimension_semantics=("arbitrary", "arbitrary"), vmem_limit_bytes=VMEM_LIMIT),
        name="conv_proj",
    )(x2, g.reshape(1, d), w_all, w_all, w_all, w_all, conv_w, conv_b.reshape(1, WIDTH))


def _attn_kernel(qn_ref, kn_ref, vn_ref, zn_ref, qc_ref, kc_ref, vc_ref, o_ref,
                 o4, lse4, o16, lse16):
    seq = qn_ref.shape[1]
    t = ATTN_TILE
    e = HEAD_DIM
    n_cls = qc_ref.shape[0]
    row = lax.broadcasted_iota(jnp.int32, (t, t), 0)
    col = lax.broadcasted_iota(jnp.int32, (t, t), 1)

    def bdot(a, b, contract_b):
        return lax.dot_general(a, b, (((2,), (contract_b,)), ((0,), (0,))),
                               preferred_element_type=F32)

    def partials(q, k_cur, k_prev, v_cur, v_prev, cur_ok, prev_ok):
        ones = jnp.ones(v_cur.shape, BF16)
        s_c = jnp.where(cur_ok, bdot(q, k_cur, 2), NEG_BIG)
        s_p = jnp.where(prev_ok, bdot(q, k_prev, 2), NEG_BIG)
        m = jnp.max(jnp.maximum(s_c, s_p), axis=-1, keepdims=True)
        p_c = jnp.exp2(s_c - m).astype(BF16)
        p_p = jnp.exp2(s_p - m).astype(BF16)
        acc_l = (bdot(p_c, jnp.concatenate([v_cur, ones], axis=2), 1)
                 + bdot(p_p, jnp.concatenate([v_prev, ones], axis=2), 1))
        return acc_l[:, :, :e], m, acc_l[:, :, e:]

    def previous_tiles(x):
        return jnp.concatenate([x[:, :1], x[:, :-1]], axis=1)

    def dilated_pass(dil, o_s, lse_s):
        g = ATTN_BATCH[dil]
        step = dil // n_cls
        tiles = seq // (dil * t)
        classes = g // tiles
        slot = lax.broadcasted_iota(jnp.int32, (g, t, t), 0)
        cur_ok = col <= row
        prev_ok = jnp.logical_and(col >= row, slot % tiles > 0)

        def class_rows(ref, r):
            if step == 1:
                return ref[r]
            return ref[r % n_cls, pl.ds(r // n_cls, tiles * t, stride=step), :]

        def gather(ref, r0):
            return jnp.stack([class_rows(ref, r0 + c).astype(BF16).reshape(tiles, t, e)
                              for c in range(classes)], axis=0)

        def body(i, carry):
            r0 = i * classes
            k, v = gather(kc_ref, r0), gather(vc_ref, r0)
            acc, m, l = partials(gather(qc_ref, r0).reshape(g, t, e),
                                 k.reshape(g, t, e), previous_tiles(k).reshape(g, t, e),
                                 v.reshape(g, t, e), previous_tiles(v).reshape(g, t, e),
                                 cur_ok, prev_ok)
            out = acc / l
            lse = m + jnp.log2(l)
            for c in range(classes):
                dst = pl.ds(r0 + c, tiles * t, stride=dil)
                part = lambda x: x[c * tiles:(c + 1) * tiles].reshape(tiles * t, e)
                o_s[dst, :] = part(out)
                lse_s[dst, :] = part(lse)
            return carry

        lax.fori_loop(0, dil // classes, body, 0)

    dilated_pass(4, o4, lse4)
    dilated_pass(16, o16, lse16)

    g = ATTN_BATCH[1]
    slot = lax.broadcasted_iota(jnp.int32, (g, t, t), 0)
    for i in range(seq // (g * t)):
        start = i * g * t
        rows = slice(start, start + g * t)
        if i == 0:
            with_prev = lambda ref: jnp.concatenate([ref[0, :t], ref[0, rows]], axis=0)
            prev_ok = jnp.logical_and(col >= row, slot > 0)
        else:
            with_prev = lambda ref: ref[0, start - t:start + g * t]
            prev_ok = col >= row
        k_all, v_all = with_prev(kn_ref), with_prev(vn_ref)
        acc1, m1, l1 = partials(qn_ref[0, rows].reshape(g, t, e),
                                k_all[t:].reshape(g, t, e), k_all[:g * t].reshape(g, t, e),
                                v_all[t:].reshape(g, t, e), v_all[:g * t].reshape(g, t, e),
                                col <= row, prev_ok)
        load = lambda ref: ref[rows, :].reshape(g, t, e)
        mm4, mm16 = load(lse4), load(lse16)
        m = jnp.maximum(jnp.maximum(m1, mm4), mm16)
        w1 = jnp.exp2(m1 - m)
        w4 = jnp.exp2(mm4 - m)
        w16 = jnp.exp2(mm16 - m)
        num = w1 * acc1 + w4 * load(o4) + w16 * load(o16)
        den = w1 * l1 + w4 + w16
        z = zn_ref[0, rows].astype(F32).reshape(g, t, e)
        o_ref[rows, :] = (num / den * _silu(z)).astype(BF16).reshape(g * t, e)


def _attention(nat, cm, batch, seq):
    m = nat.shape[1]

    def spec(base):
        return pl.BlockSpec((1, seq, LANE), lambda b, h: (base + h, b, 0))

    def cm_spec(base):
        return pl.BlockSpec((None, None, N_CLASSES, seq // N_CLASSES, LANE),
                            lambda b, h: (base + h, b, 0, 0, 0))

    return pl.pallas_call(
        _attn_kernel,
        grid=(batch, N_HEADS),
        in_specs=[spec(SLAB_QA), spec(SLAB_KA), spec(SLAB_VA), spec(SLAB_ZA),
                  cm_spec(SLAB_QA), cm_spec(SLAB_KA), cm_spec(SLAB_VA)],
        out_specs=pl.BlockSpec((seq, LANE), lambda b, h: (b, h)),
        out_shape=jax.ShapeDtypeStruct((m, WIDTH), BF16),
        scratch_shapes=[pltpu.VMEM((seq, LANE), F32)] * 4,
        compiler_params=pltpu.CompilerParams(
            dimension_semantics=("parallel", "parallel"), vmem_limit_bytes=VMEM_LIMIT),
        name="dilated_attn",
    )(nat, nat, nat, nat, cm, cm, cm)


def _ret_kernel(lg_ref, q_ref, k_ref, v_ref, z_ref, cos_ref, sin_ref, o_ref):
    seq = q_ref.shape[1]
    c = RET_CHUNK
    lg = lg_ref[pl.program_id(1)]
    row = lax.broadcasted_iota(jnp.int32, (c, c), 0).astype(F32)
    col = lax.broadcasted_iota(jnp.int32, (c, c), 1).astype(F32)
    rel = row - col
    decay_mask = jnp.where(rel >= 0, jnp.exp(jnp.maximum(rel, 0.0) * lg), 0.0)
    q_decay = jnp.exp((row + 1.0) * lg)
    k_decay = jnp.exp((c - 1.0 - row) * lg)
    chunk_decay = jnp.exp(jnp.full((c, c), float(c), F32) * lg)

    g = RET_BATCH
    e = HEAD_DIM

    def bdot(a, b, contract_a, contract_b):
        return lax.dot_general(a, b, (((contract_a,), (contract_b,)), ((0,), (0,))),
                               preferred_element_type=F32)

    def body(i, state):
        sl = pl.ds(pl.multiple_of(i * (g * c), g * c), g * c)
        cos, sin = cos_ref[sl, :], sin_ref[sl, :]
        rotate = lambda x: x * cos + pltpu.roll(x, e // 2, 1) * sin
        q = rotate(q_ref[0, sl, :].astype(F32)).reshape(g, c, e)
        k = (rotate(k_ref[0, sl, :].astype(F32)) * (HEAD_DIM ** -0.5)).reshape(g, c, e)
        v = v_ref[0, sl, :].reshape(g, c, e)
        qb = q.astype(BF16)
        inner = bdot(qb, k.astype(BF16), 2, 2) * decay_mask
        o = bdot(inner.astype(BF16), v, 2, 1)
        kv = bdot((k * k_decay).astype(BF16), v, 1, 1)
        states = []
        for j in range(g):
            states.append(state)
            state = state * chunk_decay + kv[j]
        o = o + bdot(qb, jnp.stack(states).astype(BF16), 2, 1) * q_decay
        o = o * lax.rsqrt(jnp.mean(o * o, axis=-1, keepdims=True) + NORM_EPS)
        z = z_ref[0, sl, :].astype(F32).reshape(g, c, e)
        o_ref[sl, :] = (o * _silu(z)).astype(BF16).reshape(g * c, e)
        return state

    lax.fori_loop(0, seq // (g * c), body, jnp.zeros((e, e), F32))


def _retention(proj, log_gamma, cos2, sin2, batch, seq):
    m = proj.shape[1]

    def spec(base):
        return pl.BlockSpec((1, seq, LANE), lambda b, h, lg: (base + h, b, 0))

    table = pl.BlockSpec((seq, LANE), lambda b, h, lg: (0, 0))
    return pl.pallas_call(
        _ret_kernel,
        grid_spec=pltpu.PrefetchScalarGridSpec(
            num_scalar_prefetch=1,
            grid=(batch, N_HEADS),
            in_specs=[spec(SLAB_QR), spec(SLAB_KR), spec(SLAB_VR), spec(SLAB_ZR), table, table],
            out_specs=pl.BlockSpec((seq, LANE), lambda b, h, lg: (b, h)),
        ),
        out_shape=jax.ShapeDtypeStruct((m, WIDTH), BF16),
        compiler_params=pltpu.CompilerParams(
            dimension_semantics=("parallel", "parallel"), vmem_limit_bytes=VMEM_LIMIT),
        name="retention",
    )(log_gamma, proj, proj, proj, proj, cos2, sin2)


def _merge_kernel(oa_ref, oc_ref, or_ref, ga_ref, gc_ref, gr_ref, wa_ref, wc_ref, wr_ref, o_ref):
    ya = _dot(oa_ref[...], wa_ref[...].astype(BF16))
    yc = _dot(oc_ref[...], wc_ref[...].astype(BF16))
    yr = _dot(or_ref[...], wr_ref[...].astype(BF16))
    for s in range(ga_ref.shape[0]):
        lanes = slice(s * LANE, (s + 1) * LANE)
        merged = (_sigmoid(ga_ref[s].astype(F32)) * ya[:, lanes]
                  + _sigmoid(gc_ref[s].astype(F32)) * yc[:, lanes]
                  + _sigmoid(gr_ref[s].astype(F32)) * yr[:, lanes])
        o_ref[:, lanes] = merged.astype(BF16)


def _merge(proj, oa, oc, orr, wa, wc, wr, layer, tm=1024, tn=512):
    m = proj.shape[1]
    gs = tn // LANE

    def gate(base):
        return pl.BlockSpec((gs, tm, LANE), lambda i, n: (base // gs + n, i, 0))

    wspec = pl.BlockSpec((None, WIDTH, tn), lambda i, n: (layer, 0, n))
    own = pl.BlockSpec((tm, WIDTH), lambda i, n: (i, 0))
    return pl.pallas_call(
        _merge_kernel,
        grid=(m // tm, D_MODEL // tn),
        in_specs=[own, own, own, gate(SLAB_GA), gate(SLAB_GC), gate(SLAB_GR), wspec, wspec, wspec],
        out_specs=pl.BlockSpec((tm, tn), lambda i, n: (i, n)),
        out_shape=jax.ShapeDtypeStruct((m, D_MODEL), BF16),
        compiler_params=pltpu.CompilerParams(
            dimension_semantics=("parallel", "parallel"), vmem_limit_bytes=VMEM_LIMIT),
        name="branch_merge",
    )(oa, oc, orr, proj, proj, proj, wa, wc, wr)


def _out_proj_kernel(m_ref, w_ref, x_ref, g_ref, o_ref):
    y = _dot(m_ref[...], w_ref[...].astype(BF16))
    ms = jnp.mean(y * y, axis=-1, keepdims=True)
    o_ref[...] = x_ref[...] + y * lax.rsqrt(ms + NORM_EPS) * g_ref[...]


def _out_proj(merged, w_all, layer, x2, g, tm=512):
    m, d = x2.shape
    return pl.pallas_call(
        _out_proj_kernel,
        grid=(m // tm,),
        in_specs=[pl.BlockSpec((tm, d), lambda i: (i, 0)),
                  pl.BlockSpec((None, d, d), lambda i: (layer, 0, 0), pipeline_mode=pl.Buffered(1)),
                  pl.BlockSpec((tm, d), lambda i: (i, 0)),
                  pl.BlockSpec((1, d), lambda i: (0, 0))],
        out_specs=pl.BlockSpec((tm, d), lambda i: (i, 0)),
        out_shape=jax.ShapeDtypeStruct((m, d), F32),
        compiler_params=pltpu.CompilerParams(
            dimension_semantics=("parallel",), vmem_limit_bytes=VMEM_LIMIT),
        name="out_proj",
    )(merged, w_all, x2, g.reshape(1, d))


def _rotary_tables(seq):
    half = HEAD_DIM // 2
    inv_freq = ROPE_BASE ** (-jnp.arange(half, dtype=F32) / half)
    ang = jnp.arange(seq, dtype=F32)[:, None] * inv_freq[None, :]
    cos, sin = jnp.cos(ang), jnp.sin(ang)
    return jnp.concatenate([cos, cos], axis=-1), jnp.concatenate([-sin, sin], axis=-1)


def kernel(x, pre_norm_g, post_norm_g, w_in, conv_w, conv_b, w_branch_a, w_branch_c, w_branch_r, w_out):
    batch, seq, d = x.shape
    assert d == D_MODEL and w_in.shape[-1] == N_IN
    assert seq % (max(DILATIONS) * ATTN_TILE) == 0 and seq % RET_CHUNK == 0
    cos2, sin2 = _rotary_tables(seq)
    log_gamma = jnp.log1p(-jnp.exp2(-5.0 - jnp.arange(N_HEADS, dtype=F32)))
    x2 = x.reshape(batch * seq, d)
    for layer in range(w_in.shape[0]):
        attn_nat, attn_cm = _attn_proj(x2, pre_norm_g[layer], w_in, layer, batch, seq)
        oa = _attention(attn_nat, attn_cm, batch, seq)
        proj = _in_proj(x2, pre_norm_g[layer], w_in, layer, COL_QR, N_IN // LANE - COL_QR)
        orr = _retention(proj, log_gamma, cos2, sin2, batch, seq)
        oc = _conv_proj(x2, pre_norm_g[layer], w_in, conv_w[layer], conv_b[layer], layer, seq)
        merged = _merge(proj, oa, oc, orr, w_branch_a, w_branch_c, w_branch_r, layer)
        x2 = _out_proj(merged, w_out, layer, x2, post_norm_g[layer])
    return x2.reshape(batch, seq, d)
```

```python
import functools
import math

import jax
import jax.numpy as jnp
from jax import lax
from jax.experimental import pallas as pl
from jax.experimental.pallas import tpu as pltpu

F32 = jnp.float32
BF16 = jnp.bfloat16

LANE = 128
D_MODEL = 2048
HEAD_DIM = 128
N_HEADS = 12
WIDTH = N_HEADS * HEAD_DIM
N_IN = 12 * WIDTH + 3 * D_MODEL
DILATIONS = (1, 4, 16)
ATTN_TILE = 128
ATTN_BATCH = {1: 16, 4: 16, 16: 16}
LOG2_E = math.log2(math.e)
RET_CHUNK = 128
RET_BATCH = 8
CONV_K = 3
ROPE_BASE = 10000.0
NORM_EPS = 1e-6
NEG_BIG = -1e30

COL_QA, COL_KA, COL_VA, COL_ZA = 0, 12, 24, 36
COL_UC, COL_BC, COL_CC, COL_ZC = 48, 60, 72, 84
COL_QR, COL_KR, COL_VR, COL_ZR = 96, 108, 120, 132
COL_GA, COL_GC, COL_GR = 144, 160, 176
ATTN_SLABS = COL_UC - COL_QA
SLAB_QA, SLAB_KA, SLAB_VA, SLAB_ZA = COL_QA, COL_KA, COL_VA, COL_ZA
SLAB_QR, SLAB_KR, SLAB_VR, SLAB_ZR = (c - COL_QR for c in (COL_QR, COL_KR, COL_VR, COL_ZR))
SLAB_GA, SLAB_GC, SLAB_GR = (c - COL_QR for c in (COL_GA, COL_GC, COL_GR))
N_CLASSES = 4

VMEM_LIMIT = 56 * 1024 * 1024


def _sigmoid(x):
    return 0.5 * jnp.tanh(0.5 * x) + 0.5


def _silu(x):
    return x * _sigmoid(x)


def _dot(a, b):
    return jnp.dot(a, b, preferred_element_type=F32)


def _pre_norm_kernel(x_ref, g_ref, o_ref):
    xf = x_ref[...]
    ms = jnp.mean(xf * xf, axis=-1, keepdims=True)
    o_ref[...] = (xf * lax.rsqrt(ms + NORM_EPS) * g_ref[...]).astype(BF16)


def _pre_norm(x2, g, tm=512):
    m, d = x2.shape
    return pl.pallas_call(
        _pre_norm_kernel,
        grid=(m // tm,),
        in_specs=[pl.BlockSpec((tm, d), lambda i: (i, 0)), pl.BlockSpec((1, d), lambda i: (0, 0))],
        out_specs=pl.BlockSpec((tm, d), lambda i: (i, 0)),
        out_shape=jax.ShapeDtypeStruct((m, d), BF16),
        compiler_params=pltpu.CompilerParams(
            dimension_semantics=("parallel",), vmem_limit_bytes=VMEM_LIMIT),
        name="pre_norm",
    )(x2, g.reshape(1, d))


def _in_proj_kernel(h_ref, w_ref, o_ref):
    acc = _dot(h_ref[...], w_ref[...].astype(BF16))
    for c in range(o_ref.shape[0]):
        o_ref[c] = acc[:, c * LANE:(c + 1) * LANE].astype(BF16)


def _in_proj(h, w_all, layer, col0, n_slabs, tm=1024, tn=1024):
    m, d = h.shape
    tile0 = col0 * LANE // tn
    return pl.pallas_call(
        _in_proj_kernel,
        grid=(n_slabs * LANE // tn, m // tm),
        in_specs=[
            pl.BlockSpec((tm, d), lambda j, i: (i, 0)),
            pl.BlockSpec((None, d, tn), lambda j, i: (layer, 0, tile0 + j)),
        ],
        out_specs=pl.BlockSpec((tn // LANE, tm, LANE), lambda j, i: (j, i, 0)),
        out_shape=jax.ShapeDtypeStruct((n_slabs, m, LANE), BF16),
        compiler_params=pltpu.CompilerParams(
            dimension_semantics=("parallel", "parallel"), vmem_limit_bytes=VMEM_LIMIT),
        name="in_proj",
    )(h, w_all)


def _attn_proj_kernel(h_ref, w_ref, nat_ref, cm_ref, scr_ref):
    j = pl.program_id(0)
    n_slabs, n_cls, rows = cm_ref.shape[0], cm_ref.shape[1], cm_ref.shape[2]
    pair = 2 * LANE
    for c0 in range(0, n_slabs, 2):
        acc = _dot(h_ref[...], w_ref[:, c0 * LANE:c0 * LANE + pair].astype(BF16))
        for c in (c0, c0 + 1):
            is_q = (j * n_slabs + c) < N_HEADS
            slab = (acc[:, (c - c0) * LANE:(c - c0 + 1) * LANE]
                    * jnp.where(is_q, HEAD_DIM ** -0.5 * LOG2_E, 1.0))
            nat_ref[c] = slab.astype(BF16)
            scr_ref[c] = slab
            for r in range(n_cls):
                cm_ref[c, r] = scr_ref[c, pl.ds(r, rows, stride=n_cls), :]


def _attn_proj(h, w_all, layer, batch, seq, tm=1024, tn=512):
    m, d = h.shape
    tiles_per_seq = seq // tm
    slabs = tn // LANE
    return pl.pallas_call(
        _attn_proj_kernel,
        grid=(ATTN_SLABS * LANE // tn, m // tm),
        in_specs=[
            pl.BlockSpec((tm, d), lambda j, i: (i, 0)),
            pl.BlockSpec((None, d, tn), lambda j, i: (layer, 0, COL_QA * LANE // tn + j)),
        ],
        out_specs=[
            pl.BlockSpec((slabs, tm, LANE), lambda j, i: (j, i, 0)),
            pl.BlockSpec((slabs, None, N_CLASSES, tm // N_CLASSES, LANE),
                         lambda j, i: (j, i // tiles_per_seq, 0, i % tiles_per_seq, 0)),
        ],
        out_shape=[jax.ShapeDtypeStruct((ATTN_SLABS, m, LANE), BF16),
                   jax.ShapeDtypeStruct((ATTN_SLABS, batch, N_CLASSES, seq // N_CLASSES, LANE), F32)],
        scratch_shapes=[pltpu.VMEM((slabs, tm, LANE), F32)],
        compiler_params=pltpu.CompilerParams(
            dimension_semantics=("parallel", "parallel"), vmem_limit_bytes=VMEM_LIMIT),
        name="attn_proj",
    )(h, w_all)


def _conv_proj_kernel(h_ref, wu_ref, wb_ref, wc_ref, wz_ref, cw_ref, cb_ref, o_ref, halo_ref,
                      *, tiles_per_seq):
    i = pl.program_id(1)
    tm = h_ref.shape[0]

    @pl.when(i == 0)
    def _():
        halo_ref[...] = jnp.zeros(halo_ref.shape, F32)

    h = h_ref[...]
    cu = _dot(h, wc_ref[...].astype(BF16)) * _dot(h, wu_ref[...].astype(BF16))
    halo = jnp.where((i % tiles_per_seq) == 0, 0.0, halo_ref[...])
    halo_ref[...] = cu[tm - 8:, :]
    row = lax.broadcasted_iota(jnp.int32, (8, cu.shape[1]), 0)
    back1 = pltpu.roll(cu, 1, 0)
    back2 = pltpu.roll(cu, 2, 0)
    head1 = jnp.where(row == 0, halo[7:8, :], back1[:8])
    head2 = jnp.where(row == 0, halo[6:7, :], jnp.where(row == 1, halo[7:8, :], back2[:8]))
    back1 = jnp.concatenate([head1, back1[8:]], axis=0)
    back2 = jnp.concatenate([head2, back2[8:]], axis=0)
    conv = (cb_ref[...] + cw_ref[2:3, :] * cu + cw_ref[0:1, :] * back2 + cw_ref[1:2, :] * back1)
    hz = 0.5 * _dot(h, wz_ref[...].astype(BF16))
    gated = (conv * (_dot(h, wb_ref[...].astype(BF16)) * hz)) * (1.0 + jnp.tanh(hz))
    o_ref[...] = gated.astype(BF16)


def _conv_proj(h, w_all, conv_w, conv_b, layer, seq, tm=1024, tc=256):
    m, d = h.shape

    def wspec(col):
        return pl.BlockSpec((None, d, tc), lambda jc, i: (layer, 0, col * LANE // tc + jc))

    return pl.pallas_call(
        functools.partial(_conv_proj_kernel, tiles_per_seq=seq // tm),
        grid=(WIDTH // tc, m // tm),
        in_specs=[pl.BlockSpec((tm, d), lambda jc, i: (i, 0)),
                  wspec(COL_UC), wspec(COL_BC), wspec(COL_CC), wspec(COL_ZC),
                  pl.BlockSpec((CONV_K, tc), lambda jc, i: (0, jc)),
                  pl.BlockSpec((1, tc), lambda jc, i: (0, jc))],
        out_specs=pl.BlockSpec((tm, tc), lambda jc, i: (i, jc)),
        out_shape=jax.ShapeDtypeStruct((m, WIDTH), BF16),
        scratch_shapes=[pltpu.VMEM((8, tc), F32)],
        compiler_params=pltpu.CompilerParams(
            dimension_semantics=("parallel", "arbitrary"), vmem_limit_bytes=VMEM_LIMIT),
        name="conv_proj",
    )(h, w_all, w_all, w_all, w_all, conv_w, conv_b.reshape(1, WIDTH))


def _attn_kernel(qn_ref, kn_ref, vn_ref, zn_ref, qc_ref, kc_ref, vc_ref, o_ref,
                 o4, lse4, o16, lse16):
    seq = qn_ref.shape[1]
    t = ATTN_TILE
    e = HEAD_DIM
    n_cls = qc_ref.shape[0]
    row = lax.broadcasted_iota(jnp.int32, (t, t), 0)
    col = lax.broadcasted_iota(jnp.int32, (t, t), 1)

    def bdot(a, b, contract_b):
        return lax.dot_general(a, b, (((2,), (contract_b,)), ((0,), (0,))),
                               preferred_element_type=F32)

    def partials(q, k_cur, k_prev, v_cur, v_prev, cur_ok, prev_ok):
        ones = jnp.ones(v_cur.shape, BF16)
        s_c = jnp.where(cur_ok, bdot(q, k_cur, 2), NEG_BIG)
        s_p = jnp.where(prev_ok, bdot(q, k_prev, 2), NEG_BIG)
        m = jnp.max(jnp.maximum(s_c, s_p), axis=-1, keepdims=True)
        p_c = jnp.exp2(s_c - m).astype(BF16)
        p_p = jnp.exp2(s_p - m).astype(BF16)
        acc_l = (bdot(p_c, jnp.concatenate([v_cur, ones], axis=2), 1)
                 + bdot(p_p, jnp.concatenate([v_prev, ones], axis=2), 1))
        return acc_l[:, :, :e], m, acc_l[:, :, e:]

    def previous_tiles(x):
        return jnp.concatenate([x[:, :1], x[:, :-1]], axis=1)

    def dilated_pass(dil, o_s, lse_s):
        g = ATTN_BATCH[dil]
        step = dil // n_cls
        tiles = seq // (dil * t)
        classes = g // tiles
        slot = lax.broadcasted_iota(jnp.int32, (g, t, t), 0)
        cur_ok = col <= row
        prev_ok = jnp.logical_and(col >= row, slot % tiles > 0)

        def class_rows(ref, r):
            if step == 1:
                return ref[r]
            return ref[r % n_cls, pl.ds(r // n_cls, tiles * t, stride=step), :]

        def gather(ref, r0):
            return jnp.stack([class_rows(ref, r0 + c).astype(BF16).reshape(tiles, t, e)
                              for c in range(classes)], axis=0)

        def body(i, carry):
            r0 = i * classes
            k, v = gather(kc_ref, r0), gather(vc_ref, r0)
            acc, m, l = partials(gather(qc_ref, r0).reshape(g, t, e),
                                 k.reshape(g, t, e), previous_tiles(k).reshape(g, t, e),
                                 v.reshape(g, t, e), previous_tiles(v).reshape(g, t, e),
                                 cur_ok, prev_ok)
            out = acc / l
            lse = m + jnp.log2(l)
            for c in range(classes):
                dst = pl.ds(r0 + c, tiles * t, stride=dil)
                part = lambda x: x[c * tiles:(c + 1) * tiles].reshape(tiles * t, e)
                o_s[dst, :] = part(out)
                lse_s[dst, :] = part(lse)
            return carry

        lax.fori_loop(0, dil // classes, body, 0)

    dilated_pass(4, o4, lse4)
    dilated_pass(16, o16, lse16)

    g = ATTN_BATCH[1]
    slot = lax.broadcasted_iota(jnp.int32, (g, t, t), 0)
    for i in range(seq // (g * t)):
        start = i * g * t
        rows = slice(start, start + g * t)
        if i == 0:
            with_prev = lambda ref: jnp.concatenate([ref[0, :t], ref[0, rows]], axis=0)
            prev_ok = jnp.logical_and(col >= row, slot > 0)
        else:
            with_prev = lambda ref: ref[0, start - t:start + g * t]
            prev_ok = col >= row
        k_all, v_all = with_prev(kn_ref), with_prev(vn_ref)
        acc1, m1, l1 = partials(qn_ref[0, rows].reshape(g, t, e),
                                k_all[t:].reshape(g, t, e), k_all[:g * t].reshape(g, t, e),
                                v_all[t:].reshape(g, t, e), v_all[:g * t].reshape(g, t, e),
                                col <= row, prev_ok)
        load = lambda ref: ref[rows, :].reshape(g, t, e)
        mm4, mm16 = load(lse4), load(lse16)
        m = jnp.maximum(jnp.maximum(m1, mm4), mm16)
        w1 = jnp.exp2(m1 - m)
        w4 = jnp.exp2(mm4 - m)
        w16 = jnp.exp2(mm16 - m)
        num = w1 * acc1 + w4 * load(o4) + w16 * load(o16)
        den = w1 * l1 + w4 + w16
        z = zn_ref[0, rows].astype(F32).reshape(g, t, e)
        o_ref[rows, :] = (num / den * _silu(z)).astype(BF16).reshape(g * t, e)


def _attention(nat, cm, batch, seq):
    m = nat.shape[1]

    def spec(base):
        return pl.BlockSpec((1, seq, LANE), lambda b, h: (base + h, b, 0))

    def cm_spec(base):
        return pl.BlockSpec((None, None, N_CLASSES, seq // N_CLASSES, LANE),
                            lambda b, h: (base + h, b, 0, 0, 0))

    return pl.pallas_call(
        _attn_kernel,
        grid=(batch, N_HEADS),
        in_specs=[spec(SLAB_QA), spec(SLAB_KA), spec(SLAB_VA), spec(SLAB_ZA),
                  cm_spec(SLAB_QA), cm_spec(SLAB_KA), cm_spec(SLAB_VA)],
        out_specs=pl.BlockSpec((seq, LANE), lambda b, h: (b, h)),
        out_shape=jax.ShapeDtypeStruct((m, WIDTH), BF16),
        scratch_shapes=[pltpu.VMEM((seq, LANE), F32)] * 4,
        compiler_params=pltpu.CompilerParams(
            dimension_semantics=("parallel", "parallel"), vmem_limit_bytes=VMEM_LIMIT),
        name="dilated_attn",
    )(nat, nat, nat, nat, cm, cm, cm)


def _ret_kernel(lg_ref, q_ref, k_ref, v_ref, z_ref, cos_ref, sin_ref, o_ref):
    seq = q_ref.shape[1]
    c = RET_CHUNK
    lg = lg_ref[pl.program_id(1)]
    row = lax.broadcasted_iota(jnp.int32, (c, c), 0).astype(F32)
    col = lax.broadcasted_iota(jnp.int32, (c, c), 1).astype(F32)
    rel = row - col
    decay_mask = jnp.where(rel >= 0, jnp.exp(jnp.maximum(rel, 0.0) * lg), 0.0)
    q_decay = jnp.exp((row + 1.0) * lg)
    k_decay = jnp.exp((c - 1.0 - row) * lg)
    chunk_decay = jnp.exp(jnp.full((c, c), float(c), F32) * lg)

    g = RET_BATCH
    e = HEAD_DIM

    def bdot(a, b, contract_a, contract_b):
        return lax.dot_general(a, b, (((contract_a,), (contract_b,)), ((0,), (0,))),
                               preferred_element_type=F32)

    def body(i, state):
        sl = pl.ds(pl.multiple_of(i * (g * c), g * c), g * c)
        cos, sin = cos_ref[sl, :], sin_ref[sl, :]
        rotate = lambda x: x * cos + pltpu.roll(x, e // 2, 1) * sin
        q = rotate(q_ref[0, sl, :].astype(F32)).reshape(g, c, e)
        k = (rotate(k_ref[0, sl, :].astype(F32)) * (HEAD_DIM ** -0.5)).reshape(g, c, e)
        v = v_ref[0, sl, :].reshape(g, c, e)
        qb = q.astype(BF16)
        inner = bdot(qb, k.astype(BF16), 2, 2) * decay_mask
        o = bdot(inner.astype(BF16), v, 2, 1)
        kv = bdot((k * k_decay).astype(BF16), v, 1, 1)
        states = []
        for j in range(g):
            states.append(state)
            state = state * chunk_decay + kv[j]
        o = o + bdot(qb, jnp.stack(states).astype(BF16), 2, 1) * q_decay
        o = o * lax.rsqrt(jnp.mean(o * o, axis=-1, keepdims=True) + NORM_EPS)
        z = z_ref[0, sl, :].astype(F32).reshape(g, c, e)
        o_ref[sl, :] = (o * _silu(z)).astype(BF16).reshape(g * c, e)
        return state

    lax.fori_loop(0, seq // (g * c), body, jnp.zeros((e, e), F32))


def _retention(proj, log_gamma, cos2, sin2, batch, seq):
    m = proj.shape[1]

    def spec(base):
        return pl.BlockSpec((1, seq, LANE), lambda b, h, lg: (base + h, b, 0))

    table = pl.BlockSpec((seq, LANE), lambda b, h, lg: (0, 0))
    return pl.pallas_call(
        _ret_kernel,
        grid_spec=pltpu.PrefetchScalarGridSpec(
            num_scalar_prefetch=1,
            grid=(batch, N_HEADS),
            in_specs=[spec(SLAB_QR), spec(SLAB_KR), spec(SLAB_VR), spec(SLAB_ZR), table, table],
            out_specs=pl.BlockSpec((seq, LANE), lambda b, h, lg: (b, h)),
        ),
        out_shape=jax.ShapeDtypeStruct((m, WIDTH), BF16),
        compiler_params=pltpu.CompilerParams(
            dimension_semantics=("parallel", "parallel"), vmem_limit_bytes=VMEM_LIMIT),
        name="retention",
    )(log_gamma, proj, proj, proj, proj, cos2, sin2)


def _merge_kernel(oa_ref, oc_ref, or_ref, ga_ref, gc_ref, gr_ref, wa_ref, wc_ref, wr_ref, o_ref):
    ya = _dot(oa_ref[...], wa_ref[...].astype(BF16))
    yc = _dot(oc_ref[...], wc_ref[...].astype(BF16))
    yr = _dot(or_ref[...], wr_ref[...].astype(BF16))
    for s in range(ga_ref.shape[0]):
        lanes = slice(s * LANE, (s + 1) * LANE)
        merged = (_sigmoid(ga_ref[s].astype(F32)) * ya[:, lanes]
                  + _sigmoid(gc_ref[s].astype(F32)) * yc[:, lanes]
                  + _sigmoid(gr_ref[s].astype(F32)) * yr[:, lanes])
        o_ref[:, lanes] = merged.astype(BF16)


def _merge(proj, oa, oc, orr, wa, wc, wr, layer, tm=1024, tn=512):
    m = proj.shape[1]
    gs = tn // LANE

    def gate(base):
        return pl.BlockSpec((gs, tm, LANE), lambda n, i: (base // gs + n, i, 0))

    wspec = pl.BlockSpec((None, WIDTH, tn), lambda n, i: (layer, 0, n))
    own = pl.BlockSpec((tm, WIDTH), lambda n, i: (i, 0))
    return pl.pallas_call(
        _merge_kernel,
        grid=(D_MODEL // tn, m // tm),
        in_specs=[own, own, own, gate(SLAB_GA), gate(SLAB_GC), gate(SLAB_GR), wspec, wspec, wspec],
        out_specs=pl.BlockSpec((tm, tn), lambda n, i: (i, n)),
        out_shape=jax.ShapeDtypeStruct((m, D_MODEL), BF16),
        compiler_params=pltpu.CompilerParams(
            dimension_semantics=("parallel", "parallel"), vmem_limit_bytes=VMEM_LIMIT),
        name="branch_merge",
    )(oa, oc, orr, proj, proj, proj, wa, wc, wr)


def _out_proj_kernel(m_ref, w_ref, x_ref, g_ref, o_ref):
    y = _dot(m_ref[...], w_ref[...].astype(BF16))
    ms = jnp.mean(y * y, axis=-1, keepdims=True)
    o_ref[...] = x_ref[...] + y * lax.rsqrt(ms + NORM_EPS) * g_ref[...]


def _out_proj(merged, w_all, layer, x2, g, tm=512):
    m, d = x2.shape
    return pl.pallas_call(
        _out_proj_kernel,
        grid=(m // tm,),
        in_specs=[pl.BlockSpec((tm, d), lambda i: (i, 0)),
                  pl.BlockSpec((None, d, d), lambda i: (layer, 0, 0), pipeline_mode=pl.Buffered(1)),
                  pl.BlockSpec((tm, d), lambda i: (i, 0)),
                  pl.BlockSpec((1, d), lambda i: (0, 0))],
        out_specs=pl.BlockSpec((tm, d), lambda i: (i, 0)),
        out_shape=jax.ShapeDtypeStruct((m, d), F32),
        compiler_params=pltpu.CompilerParams(
            dimension_semantics=("parallel",), vmem_limit_bytes=VMEM_LIMIT),
        name="out_proj",
    )(merged, w_all, x2, g.reshape(1, d))


def _rotary_tables(seq):
    half = HEAD_DIM // 2
    inv_freq = ROPE_BASE ** (-jnp.arange(half, dtype=F32) / half)
    ang = jnp.arange(seq, dtype=F32)[:, None] * inv_freq[None, :]
    cos, sin = jnp.cos(ang), jnp.sin(ang)
    return jnp.concatenate([cos, cos], axis=-1), jnp.concatenate([-sin, sin], axis=-1)


def kernel(x, pre_norm_g, post_norm_g, w_in, conv_w, conv_b, w_branch_a, w_branch_c, w_branch_r, w_out):
    batch, seq, d = x.shape
    assert d == D_MODEL and w_in.shape[-1] == N_IN
    assert seq % (max(DILATIONS) * ATTN_TILE) == 0 and seq % RET_CHUNK == 0
    cos2, sin2 = _rotary_tables(seq)
    log_gamma = jnp.log1p(-jnp.exp2(-5.0 - jnp.arange(N_HEADS, dtype=F32)))
    x2 = x.reshape(batch * seq, d)
    for layer in range(w_in.shape[0]):
        h = _pre_norm(x2, pre_norm_g[layer])
        attn_nat, attn_cm = _attn_proj(h, w_in, layer, batch, seq)
        oa = _attention(attn_nat, attn_cm, batch, seq)
        proj = _in_proj(h, w_in, layer, COL_QR, N_IN // LANE - COL_QR)
        orr = _retention(proj, log_gamma, cos2, sin2, batch, seq)
        oc = _conv_proj(h, w_in, conv_w[layer], conv_b[layer], layer, seq)
        merged = _merge(proj, oa, oc, orr, w_branch_a, w_branch_c, w_branch_r, layer)
        x2 = _out_proj(merged, w_out, layer, x2, post_norm_g[layer])
    return x2.reshape(batch, seq, d)
```

```python
import functools
import math

import jax
import jax.numpy as jnp
from jax import lax
from jax.experimental import pallas as pl
from jax.experimental.pallas import tpu as pltpu

F32 = jnp.float32
BF16 = jnp.bfloat16

LANE = 128
D_MODEL = 2048
HEAD_DIM = 128
N_HEADS = 12
WIDTH = N_HEADS * HEAD_DIM
N_IN = 12 * WIDTH + 3 * D_MODEL
DILATIONS = (1, 4, 16)
ATTN_TILE = 128
ATTN_BATCH = {1: 8, 4: 16, 16: 16}
LOG2_E = math.log2(math.e)
RET_CHUNK = 128
RET_BATCH = 8
CONV_K = 3
ROPE_BASE = 10000.0
NORM_EPS = 1e-6
NEG_BIG = -1e30

COL_QA, COL_KA, COL_VA, COL_ZA = 0, 12, 24, 36
COL_UC, COL_BC, COL_CC, COL_ZC = 48, 60, 72, 84
COL_QR, COL_KR, COL_VR, COL_ZR = 96, 108, 120, 132
COL_GA, COL_GC, COL_GR = 144, 160, 176
ATTN_SLABS = COL_UC - COL_QA
SLAB_QA, SLAB_KA, SLAB_VA, SLAB_ZA = COL_QA, COL_KA, COL_VA, COL_ZA
SLAB_QR, SLAB_KR, SLAB_VR, SLAB_ZR = (c - COL_QR for c in (COL_QR, COL_KR, COL_VR, COL_ZR))
SLAB_GA, SLAB_GC, SLAB_GR = (c - COL_QR for c in (COL_GA, COL_GC, COL_GR))
N_CLASSES = 4

VMEM_LIMIT = 56 * 1024 * 1024


def _sigmoid(x):
    return 0.5 * jnp.tanh(0.5 * x) + 0.5


def _silu(x):
    return x * _sigmoid(x)


def _dot(a, b):
    return jnp.dot(a, b, preferred_element_type=F32)


def _pre_norm_kernel(x_ref, g_ref, o_ref):
    xf = x_ref[...]
    ms = jnp.mean(xf * xf, axis=-1, keepdims=True)
    o_ref[...] = (xf * lax.rsqrt(ms + NORM_EPS) * g_ref[...]).astype(BF16)


def _pre_norm(x2, g, tm=512):
    m, d = x2.shape
    return pl.pallas_call(
        _pre_norm_kernel,
        grid=(m // tm,),
        in_specs=[pl.BlockSpec((tm, d), lambda i: (i, 0)), pl.BlockSpec((1, d), lambda i: (0, 0))],
        out_specs=pl.BlockSpec((tm, d), lambda i: (i, 0)),
        out_shape=jax.ShapeDtypeStruct((m, d), BF16),
        compiler_params=pltpu.CompilerParams(
            dimension_semantics=("parallel",), vmem_limit_bytes=VMEM_LIMIT),
        name="pre_norm",
    )(x2, g.reshape(1, d))


def _in_proj_kernel(h_ref, w_ref, o_ref):
    acc = _dot(h_ref[...], w_ref[...].astype(BF16))
    for c in range(o_ref.shape[0]):
        o_ref[c] = acc[:, c * LANE:(c + 1) * LANE].astype(BF16)


def _in_proj(h, w_all, layer, col0, n_slabs, tm=1024, tn=1024):
    m, d = h.shape
    tile0 = col0 * LANE // tn
    return pl.pallas_call(
        _in_proj_kernel,
        grid=(n_slabs * LANE // tn, m // tm),
        in_specs=[
            pl.BlockSpec((tm, d), lambda j, i: (i, 0)),
            pl.BlockSpec((None, d, tn), lambda j, i: (layer, 0, tile0 + j)),
        ],
        out_specs=pl.BlockSpec((tn // LANE, tm, LANE), lambda j, i: (j, i, 0)),
        out_shape=jax.ShapeDtypeStruct((n_slabs, m, LANE), BF16),
        compiler_params=pltpu.CompilerParams(
            dimension_semantics=("parallel", "parallel"), vmem_limit_bytes=VMEM_LIMIT),
        name="in_proj",
    )(h, w_all)


def _attn_proj_kernel(h_ref, w_ref, nat_ref, cm_ref, scr_ref):
    j = pl.program_id(0)
    n_slabs, n_cls, rows = cm_ref.shape[0], cm_ref.shape[1], cm_ref.shape[2]
    pair = 2 * LANE
    for c0 in range(0, n_slabs, 2):
        acc = _dot(h_ref[...], w_ref[:, c0 * LANE:c0 * LANE + pair].astype(BF16))
        for c in (c0, c0 + 1):
            is_q = (j * n_slabs + c) < N_HEADS
            slab = (acc[:, (c - c0) * LANE:(c - c0 + 1) * LANE]
                    * jnp.where(is_q, HEAD_DIM ** -0.5 * LOG2_E, 1.0))
            nat_ref[c] = slab.astype(BF16)
            scr_ref[c] = slab
            for r in range(n_cls):
                cm_ref[c, r] = scr_ref[c, pl.ds(r, rows, stride=n_cls), :]


def _attn_proj(h, w_all, layer, batch, seq, tm=1024, tn=1024):
    m, d = h.shape
    tiles_per_seq = seq // tm
    slabs = tn // LANE
    return pl.pallas_call(
        _attn_proj_kernel,
        grid=(ATTN_SLABS * LANE // tn, m // tm),
        in_specs=[
            pl.BlockSpec((tm, d), lambda j, i: (i, 0)),
            pl.BlockSpec((None, d, tn), lambda j, i: (layer, 0, COL_QA * LANE // tn + j)),
        ],
        out_specs=[
            pl.BlockSpec((slabs, tm, LANE), lambda j, i: (j, i, 0)),
            pl.BlockSpec((slabs, None, N_CLASSES, tm // N_CLASSES, LANE),
                         lambda j, i: (j, i // tiles_per_seq, 0, i % tiles_per_seq, 0)),
        ],
        out_shape=[jax.ShapeDtypeStruct((ATTN_SLABS, m, LANE), BF16),
                   jax.ShapeDtypeStruct((ATTN_SLABS, batch, N_CLASSES, seq // N_CLASSES, LANE), F32)],
        scratch_shapes=[pltpu.VMEM((slabs, tm, LANE), F32)],
        compiler_params=pltpu.CompilerParams(
            dimension_semantics=("parallel", "parallel"), vmem_limit_bytes=VMEM_LIMIT),
        name="attn_proj",
    )(h, w_all)


def _conv_proj_kernel(h_ref, wu_ref, wb_ref, wc_ref, wz_ref, cw_ref, cb_ref, o_ref, halo_ref,
                      *, tiles_per_seq):
    i = pl.program_id(1)
    tm = h_ref.shape[0]

    @pl.when(i == 0)
    def _():
        halo_ref[...] = jnp.zeros(halo_ref.shape, F32)

    h = h_ref[...]
    cu = _dot(h, wc_ref[...].astype(BF16)) * _dot(h, wu_ref[...].astype(BF16))
    halo = jnp.where((i % tiles_per_seq) == 0, 0.0, halo_ref[...])
    halo_ref[...] = cu[tm - 8:, :]
    row = lax.broadcasted_iota(jnp.int32, (8, cu.shape[1]), 0)
    back1 = pltpu.roll(cu, 1, 0)
    back2 = pltpu.roll(cu, 2, 0)
    head1 = jnp.where(row == 0, halo[7:8, :], back1[:8])
    head2 = jnp.where(row == 0, halo[6:7, :], jnp.where(row == 1, halo[7:8, :], back2[:8]))
    back1 = jnp.concatenate([head1, back1[8:]], axis=0)
    back2 = jnp.concatenate([head2, back2[8:]], axis=0)
    conv = (cb_ref[...] + cw_ref[2:3, :] * cu + cw_ref[0:1, :] * back2 + cw_ref[1:2, :] * back1)
    hz = 0.5 * _dot(h, wz_ref[...].astype(BF16))
    gated = (conv * (_dot(h, wb_ref[...].astype(BF16)) * hz)) * (1.0 + jnp.tanh(hz))
    o_ref[...] = gated.astype(BF16)


def _conv_proj(h, w_all, conv_w, conv_b, layer, seq, tm=1024, tc=256):
    m, d = h.shape

    def wspec(col):
        return pl.BlockSpec((None, d, tc), lambda jc, i: (layer, 0, col * LANE // tc + jc))

    return pl.pallas_call(
        functools.partial(_conv_proj_kernel, tiles_per_seq=seq // tm),
        grid=(WIDTH // tc, m // tm),
        in_specs=[pl.BlockSpec((tm, d), lambda jc, i: (i, 0)),
                  wspec(COL_UC), wspec(COL_BC), wspec(COL_CC), wspec(COL_ZC),
                  pl.BlockSpec((CONV_K, tc), lambda jc, i: (0, jc)),
                  pl.BlockSpec((1, tc), lambda jc, i: (0, jc))],
        out_specs=pl.BlockSpec((tm, tc), lambda jc, i: (i, jc)),
        out_shape=jax.ShapeDtypeStruct((m, WIDTH), BF16),
        scratch_shapes=[pltpu.VMEM((8, tc), F32)],
        compiler_params=pltpu.CompilerParams(
            dimension_semantics=("parallel", "arbitrary"), vmem_limit_bytes=VMEM_LIMIT),
        name="conv_proj",
    )(h, w_all, w_all, w_all, w_all, conv_w, conv_b.reshape(1, WIDTH))


def _attn_kernel(qn_ref, kn_ref, vn_ref, zn_ref, qc_ref, kc_ref, vc_ref, h_ref, w_ref,
                 o_ref, p_ref, o4, lse4, o16, lse16):
    seq = qn_ref.shape[1]
    t = ATTN_TILE
    e = HEAD_DIM
    n_cls = qc_ref.shape[0]
    row = lax.broadcasted_iota(jnp.int32, (t, t), 0)
    col = lax.broadcasted_iota(jnp.int32, (t, t), 1)

    def bdot(a, b, contract_b):
        return lax.dot_general(a, b, (((2,), (contract_b,)), ((0,), (0,))),
                               preferred_element_type=F32)

    def projection_half(half):
        n = p_ref.shape[0] // 2
        acc = _dot(h_ref[...], w_ref[:, half * n * LANE:(half + 1) * n * LANE].astype(BF16))
        for c in range(n):
            p_ref[half * n + c] = acc[:, c * LANE:(c + 1) * LANE].astype(BF16)

    def partials(q, k_cur, k_prev, v_cur, v_prev, cur_ok, prev_ok):
        ones = jnp.ones(v_cur.shape, BF16)
        projection_half(0)
        s_c = jnp.where(cur_ok, bdot(q, k_cur, 2), NEG_BIG)
        s_p = jnp.where(prev_ok, bdot(q, k_prev, 2), NEG_BIG)
        projection_half(1)
        m = jnp.max(jnp.maximum(s_c, s_p), axis=-1, keepdims=True)
        p_c = jnp.exp2(s_c - m).astype(BF16)
        p_p = jnp.exp2(s_p - m).astype(BF16)
        acc_l = (bdot(p_c, jnp.concatenate([v_cur, ones], axis=2), 1)
                 + bdot(p_p, jnp.concatenate([v_prev, ones], axis=2), 1))
        return acc_l[:, :, :e], m, acc_l[:, :, e:]

    def previous_tiles(x):
        return jnp.concatenate([x[:, :1], x[:, :-1]], axis=1)

    def dilated_pass(dil, o_s, lse_s):
        g = ATTN_BATCH[dil]
        step = dil // n_cls
        tiles = seq // (dil * t)
        classes = g // tiles
        slot = lax.broadcasted_iota(jnp.int32, (g, t, t), 0)
        cur_ok = col <= row
        prev_ok = jnp.logical_and(col >= row, slot % tiles > 0)

        def class_rows(ref, r):
            if step == 1:
                return ref[r]
            return ref[r % n_cls, pl.ds(r // n_cls, tiles * t, stride=step), :]

        def gather(ref, r0):
            return jnp.stack([class_rows(ref, r0 + c).astype(BF16).reshape(tiles, t, e)
                              for c in range(classes)], axis=0)

        def body(i, carry):
            r0 = i * classes
            k, v = gather(kc_ref, r0), gather(vc_ref, r0)
            acc, m, l = partials(gather(qc_ref, r0).reshape(g, t, e),
                                 k.reshape(g, t, e), previous_tiles(k).reshape(g, t, e),
                                 v.reshape(g, t, e), previous_tiles(v).reshape(g, t, e),
                                 cur_ok, prev_ok)
            out = acc / l
            lse = m + jnp.log2(l)
            for c in range(classes):
                dst = pl.ds(r0 + c, tiles * t, stride=dil)
                part = lambda x: x[c * tiles:(c + 1) * tiles].reshape(tiles * t, e)
                o_s[dst, :] = part(out)
                lse_s[dst, :] = part(lse)
            return carry

        return [functools.partial(body, i, 0) for i in range(dil // classes)]

    def merge_block(i):
        g = ATTN_BATCH[1]
        slot = lax.broadcasted_iota(jnp.int32, (g, t, t), 0)
        start = i * g * t
        rows = slice(start, start + g * t)
        if i == 0:
            with_prev = lambda ref: jnp.concatenate([ref[0, :t], ref[0, rows]], axis=0)
            prev_ok = jnp.logical_and(col >= row, slot > 0)
        else:
            with_prev = lambda ref: ref[0, start - t:start + g * t]
            prev_ok = col >= row
        k_all, v_all = with_prev(kn_ref), with_prev(vn_ref)
        acc1, m1, l1 = partials(qn_ref[0, rows].reshape(g, t, e),
                                k_all[t:].reshape(g, t, e), k_all[:g * t].reshape(g, t, e),
                                v_all[t:].reshape(g, t, e), v_all[:g * t].reshape(g, t, e),
                                col <= row, prev_ok)
        load = lambda ref: ref[rows, :].reshape(g, t, e)
        mm4, mm16 = load(lse4), load(lse16)
        m = jnp.maximum(jnp.maximum(m1, mm4), mm16)
        w1 = jnp.exp2(m1 - m)
        w4 = jnp.exp2(mm4 - m)
        w16 = jnp.exp2(mm16 - m)
        num = w1 * acc1 + w4 * load(o4) + w16 * load(o16)
        den = w1 * l1 + w4 + w16
        z = zn_ref[0, rows].astype(F32).reshape(g, t, e)
        o_ref[rows, :] = (num / den * _silu(z)).astype(BF16).reshape(g * t, e)

    blocks = (dilated_pass(4, o4, lse4) + dilated_pass(16, o16, lse16)
              + [functools.partial(merge_block, i) for i in range(seq // (ATTN_BATCH[1] * t))])
    for index, block in enumerate(blocks):
        @pl.when(pl.program_id(2) == index)
        def _(block=block):
            block()


def _attn_blocks(seq):
    per_pass = lambda dil: dil // (ATTN_BATCH[dil] // (seq // (dil * ATTN_TILE)))
    return per_pass(4) + per_pass(16) + seq // (ATTN_BATCH[1] * ATTN_TILE)


def _attention_and_proj(nat, cm, h, w_all, layer, col0, n_slabs, batch, seq, tm=1024, tn=512):
    m, d = h.shape
    n_blk = _attn_blocks(seq)
    row_tiles = m // tm
    assert batch * N_HEADS * n_blk == row_tiles * (n_slabs * LANE // tn)
    tile0 = col0 * LANE // tn
    lin = lambda b, hd, k: (b * N_HEADS + hd) * n_blk + k

    def spec(base):
        return pl.BlockSpec((1, seq, LANE), lambda b, hd, k: (base + hd, b, 0))

    def cm_spec(base):
        return pl.BlockSpec((None, None, N_CLASSES, seq // N_CLASSES, LANE),
                            lambda b, hd, k: (base + hd, b, 0, 0, 0))

    return pl.pallas_call(
        _attn_kernel,
        grid=(batch, N_HEADS, n_blk),
        in_specs=[spec(SLAB_QA), spec(SLAB_KA), spec(SLAB_VA), spec(SLAB_ZA),
                  cm_spec(SLAB_QA), cm_spec(SLAB_KA), cm_spec(SLAB_VA),
                  pl.BlockSpec((tm, d), lambda b, hd, k: (lin(b, hd, k) % row_tiles, 0)),
                  pl.BlockSpec((None, d, tn),
                               lambda b, hd, k: (layer, 0, tile0 + lin(b, hd, k) // row_tiles))],
        out_specs=[pl.BlockSpec((seq, LANE), lambda b, hd, k: (b, hd)),
                   pl.BlockSpec((tn // LANE, tm, LANE),
                                lambda b, hd, k: (lin(b, hd, k) // row_tiles,
                                                  lin(b, hd, k) % row_tiles, 0))],
        out_shape=[jax.ShapeDtypeStruct((m, WIDTH), BF16),
                   jax.ShapeDtypeStruct((n_slabs, m, LANE), BF16)],
        scratch_shapes=[pltpu.VMEM((seq, LANE), F32)] * 4,
        compiler_params=pltpu.CompilerParams(
            dimension_semantics=("arbitrary", "arbitrary", "arbitrary"),
            vmem_limit_bytes=VMEM_LIMIT),
        name="attn_and_proj",
    )(nat, nat, nat, nat, cm, cm, cm, h, w_all)


def _ret_kernel(lg_ref, q_ref, k_ref, v_ref, z_ref, cos_ref, sin_ref, o_ref):
    seq = q_ref.shape[1]
    c = RET_CHUNK
    lg = lg_ref[pl.program_id(1)]
    row = lax.broadcasted_iota(jnp.int32, (c, c), 0).astype(F32)
    col = lax.broadcasted_iota(jnp.int32, (c, c), 1).astype(F32)
    rel = row - col
    decay_mask = jnp.where(rel >= 0, jnp.exp(jnp.maximum(rel, 0.0) * lg), 0.0)
    q_decay = jnp.exp((row + 1.0) * lg)
    k_decay = jnp.exp((c - 1.0 - row) * lg)
    chunk_decay = jnp.exp(jnp.full((c, c), float(c), F32) * lg)

    g = RET_BATCH
    e = HEAD_DIM

    def bdot(a, b, contract_a, contract_b):
        return lax.dot_general(a, b, (((contract_a,), (contract_b,)), ((0,), (0,))),
                               preferred_element_type=F32)

    def body(i, state):
        sl = pl.ds(pl.multiple_of(i * (g * c), g * c), g * c)
        cos, sin = cos_ref[sl, :], sin_ref[sl, :]
        rotate = lambda x: x * cos + pltpu.roll(x, e // 2, 1) * sin
        q = rotate(q_ref[0, sl, :].astype(F32)).reshape(g, c, e)
        k = (rotate(k_ref[0, sl, :].astype(F32)) * (HEAD_DIM ** -0.5)).reshape(g, c, e)
        v = v_ref[0, sl, :].reshape(g, c, e)
        qb = q.astype(BF16)
        inner = bdot(qb, k.astype(BF16), 2, 2) * decay_mask
        o = bdot(inner.astype(BF16), v, 2, 1)
        kv = bdot((k * k_decay).astype(BF16), v, 1, 1)
        states = []
        for j in range(g):
            states.append(state)
            state = state * chunk_decay + kv[j]
        o = o + bdot(qb, jnp.stack(states).astype(BF16), 2, 1) * q_decay
        o = o * lax.rsqrt(jnp.mean(o * o, axis=-1, keepdims=True) + NORM_EPS)
        z = z_ref[0, sl, :].astype(F32).reshape(g, c, e)
        o_ref[sl, :] = (o * _silu(z)).astype(BF16).reshape(g * c, e)
        return state

    lax.fori_loop(0, seq // (g * c), body, jnp.zeros((e, e), F32))


def _retention(proj, log_gamma, cos2, sin2, batch, seq):
    m = proj.shape[1]

    def spec(base):
        return pl.BlockSpec((1, seq, LANE), lambda b, h, lg: (base + h, b, 0))

    table = pl.BlockSpec((seq, LANE), lambda b, h, lg: (0, 0))
    return pl.pallas_call(
        _ret_kernel,
        grid_spec=pltpu.PrefetchScalarGridSpec(
            num_scalar_prefetch=1,
            grid=(batch, N_HEADS),
            in_specs=[spec(SLAB_QR), spec(SLAB_KR), spec(SLAB_VR), spec(SLAB_ZR), table, table],
            out_specs=pl.BlockSpec((seq, LANE), lambda b, h, lg: (b, h)),
        ),
        out_shape=jax.ShapeDtypeStruct((m, WIDTH), BF16),
        compiler_params=pltpu.CompilerParams(
            dimension_semantics=("parallel", "parallel"), vmem_limit_bytes=VMEM_LIMIT),
        name="retention",
    )(log_gamma, proj, proj, proj, proj, cos2, sin2)


def _merge_kernel(oa_ref, oc_ref, or_ref, ga_ref, gc_ref, gr_ref, wa_ref, wc_ref, wr_ref, o_ref):
    ya = _dot(oa_ref[...], wa_ref[...].astype(BF16))
    yc = _dot(oc_ref[...], wc_ref[...].astype(BF16))
    yr = _dot(or_ref[...], wr_ref[...].astype(BF16))
    for s in range(ga_ref.shape[0]):
        lanes = slice(s * LANE, (s + 1) * LANE)
        merged = (_sigmoid(ga_ref[s].astype(F32)) * ya[:, lanes]
                  + _sigmoid(gc_ref[s].astype(F32)) * yc[:, lanes]
                  + _sigmoid(gr_ref[s].astype(F32)) * yr[:, lanes])
        o_ref[:, lanes] = merged.astype(BF16)


def _merge(proj, oa, oc, orr, wa, wc, wr, layer, tm=1024, tn=512):
    m = proj.shape[1]
    gs = tn // LANE

    def gate(base):
        return pl.BlockSpec((gs, tm, LANE), lambda n, i: (base // gs + n, i, 0))

    wspec = pl.BlockSpec((None, WIDTH, tn), lambda n, i: (layer, 0, n))
    own = pl.BlockSpec((tm, WIDTH), lambda n, i: (i, 0))
    return pl.pallas_call(
        _merge_kernel,
        grid=(D_MODEL // tn, m // tm),
        in_specs=[own, own, own, gate(SLAB_GA), gate(SLAB_GC), gate(SLAB_GR), wspec, wspec, wspec],
        out_specs=pl.BlockSpec((tm, tn), lambda n, i: (i, n)),
        out_shape=jax.ShapeDtypeStruct((m, D_MODEL), BF16),
        compiler_params=pltpu.CompilerParams(
            dimension_semantics=("parallel", "parallel"), vmem_limit_bytes=VMEM_LIMIT),
        name="branch_merge",
    )(oa, oc, orr, proj, proj, proj, wa, wc, wr)


def _out_proj_kernel(m_ref, w_ref, x_ref, g_ref, o_ref):
    y = _dot(m_ref[...], w_ref[...].astype(BF16))
    ms = jnp.mean(y * y, axis=-1, keepdims=True)
    o_ref[...] = x_ref[...] + y * lax.rsqrt(ms + NORM_EPS) * g_ref[...]


def _out_proj(merged, w_all, layer, x2, g, tm=512):
    m, d = x2.shape
    return pl.pallas_call(
        _out_proj_kernel,
        grid=(m // tm,),
        in_specs=[pl.BlockSpec((tm, d), lambda i: (i, 0)),
                  pl.BlockSpec((None, d, d), lambda i: (layer, 0, 0), pipeline_mode=pl.Buffered(1)),
                  pl.BlockSpec((tm, d), lambda i: (i, 0)),
                  pl.BlockSpec((1, d), lambda i: (0, 0))],
        out_specs=pl.BlockSpec((tm, d), lambda i: (i, 0)),
        out_shape=jax.ShapeDtypeStruct((m, d), F32),
        compiler_params=pltpu.CompilerParams(
            dimension_semantics=("parallel",), vmem_limit_bytes=VMEM_LIMIT),
        name="out_proj",
    )(merged, w_all, x2, g.reshape(1, d))


def _rotary_tables(seq):
    half = HEAD_DIM // 2
    inv_freq = ROPE_BASE ** (-jnp.arange(half, dtype=F32) / half)
    ang = jnp.arange(seq, dtype=F32)[:, None] * inv_freq[None, :]
    cos, sin = jnp.cos(ang), jnp.sin(ang)
    return jnp.concatenate([cos, cos], axis=-1), jnp.concatenate([-sin, sin], axis=-1)


def kernel(x, pre_norm_g, post_norm_g, w_in, conv_w, conv_b, w_branch_a, w_branch_c, w_branch_r, w_out):
    batch, seq, d = x.shape
    assert d == D_MODEL and w_in.shape[-1] == N_IN
    assert seq % (max(DILATIONS) * ATTN_TILE) == 0 and seq % RET_CHUNK == 0
    cos2, sin2 = _rotary_tables(seq)
    log_gamma = jnp.log1p(-jnp.exp2(-5.0 - jnp.arange(N_HEADS, dtype=F32)))
    x2 = x.reshape(batch * seq, d)
    for layer in range(w_in.shape[0]):
        h = _pre_norm(x2, pre_norm_g[layer])
        attn_nat, attn_cm = _attn_proj(h, w_in, layer, batch, seq)
        oa, proj = _attention_and_proj(attn_nat, attn_cm, h, w_in, layer, COL_QR,
                                       N_IN // LANE - COL_QR, batch, seq)
        orr = _retention(proj, log_gamma, cos2, sin2, batch, seq)
        oc = _conv_proj(h, w_in, conv_w[layer], conv_b[layer], layer, seq)
        merged = _merge(proj, oa, oc, orr, w_branch_a, w_branch_c, w_branch_r, layer)
        x2 = _out_proj(merged, w_out, layer, x2, post_norm_g[layer])
    return x2.reshape(batch, seq, d)
```

```python
import functools
import math

import jax
import jax.numpy as jnp
from jax import lax
from jax.experimental import pallas as pl
from jax.experimental.pallas import tpu as pltpu

F32 = jnp.float32
BF16 = jnp.bfloat16

LANE = 128
D_MODEL = 2048
HEAD_DIM = 128
N_HEADS = 12
WIDTH = N_HEADS * HEAD_DIM
N_IN = 12 * WIDTH + 3 * D_MODEL
DILATIONS = (1, 4, 16)
ATTN_TILE = 128
ATTN_BATCH = {1: 16, 4: 16, 16: 16}
LOG2_E = math.log2(math.e)
RET_CHUNK = 128
RET_BATCH = 8
CONV_K = 3
ROPE_BASE = 10000.0
NORM_EPS = 1e-6
NEG_BIG = -1e30

COL_QA, COL_KA, COL_VA, COL_ZA = 0, 12, 24, 36
COL_UC, COL_BC, COL_CC, COL_ZC = 48, 60, 72, 84
COL_QR, COL_KR, COL_VR, COL_ZR = 96, 108, 120, 132
COL_GA, COL_GC, COL_GR = 144, 160, 176
ATTN_SLABS = COL_UC - COL_QA
SLAB_QA, SLAB_KA, SLAB_VA, SLAB_ZA = COL_QA, COL_KA, COL_VA, COL_ZA
SLAB_QR, SLAB_KR, SLAB_VR, SLAB_ZR = (c - COL_QR for c in (COL_QR, COL_KR, COL_VR, COL_ZR))
SLAB_GA, SLAB_GC, SLAB_GR = (c - COL_QR for c in (COL_GA, COL_GC, COL_GR))
N_CLASSES = 4

VMEM_LIMIT = 56 * 1024 * 1024


def _sigmoid(x):
    return 0.5 * jnp.tanh(0.5 * x) + 0.5


def _silu(x):
    return x * _sigmoid(x)


def _dot(a, b):
    return jnp.dot(a, b, preferred_element_type=F32)


def _pre_norm_kernel(x_ref, g_ref, o_ref):
    xf = x_ref[...]
    ms = jnp.mean(xf * xf, axis=-1, keepdims=True)
    o_ref[...] = (xf * lax.rsqrt(ms + NORM_EPS) * g_ref[...]).astype(BF16)


def _pre_norm(x2, g, tm=512):
    m, d = x2.shape
    return pl.pallas_call(
        _pre_norm_kernel,
        grid=(m // tm,),
        in_specs=[pl.BlockSpec((tm, d), lambda i: (i, 0)), pl.BlockSpec((1, d), lambda i: (0, 0))],
        out_specs=pl.BlockSpec((tm, d), lambda i: (i, 0)),
        out_shape=jax.ShapeDtypeStruct((m, d), BF16),
        compiler_params=pltpu.CompilerParams(
            dimension_semantics=("parallel",), vmem_limit_bytes=VMEM_LIMIT),
        name="pre_norm",
    )(x2, g.reshape(1, d))


def _in_proj_kernel(h_ref, w_ref, o_ref):
    acc = _dot(h_ref[...], w_ref[...].astype(BF16))
    for c in range(o_ref.shape[0]):
        o_ref[c] = acc[:, c * LANE:(c + 1) * LANE].astype(BF16)


def _in_proj(h, w_all, layer, col0, n_slabs, tm=1024, tn=1024):
    m, d = h.shape
    tile0 = col0 * LANE // tn
    return pl.pallas_call(
        _in_proj_kernel,
        grid=(n_slabs * LANE // tn, m // tm),
        in_specs=[
            pl.BlockSpec((tm, d), lambda j, i: (i, 0)),
            pl.BlockSpec((None, d, tn), lambda j, i: (layer, 0, tile0 + j)),
        ],
        out_specs=pl.BlockSpec((tn // LANE, tm, LANE), lambda j, i: (j, i, 0)),
        out_shape=jax.ShapeDtypeStruct((n_slabs, m, LANE), BF16),
        compiler_params=pltpu.CompilerParams(
            dimension_semantics=("parallel", "parallel"), vmem_limit_bytes=VMEM_LIMIT),
        name="in_proj",
    )(h, w_all)


def _attn_proj_kernel(h_ref, w_ref, nat_ref, cm_ref, scr_ref):
    j = pl.program_id(0)
    n_slabs, n_cls, rows = cm_ref.shape[0], cm_ref.shape[1], cm_ref.shape[2]
    pair = 2 * LANE
    for c0 in range(0, n_slabs, 2):
        acc = _dot(h_ref[...], w_ref[:, c0 * LANE:c0 * LANE + pair].astype(BF16))
        for c in (c0, c0 + 1):
            is_q = (j * n_slabs + c) < N_HEADS
            slab = (acc[:, (c - c0) * LANE:(c - c0 + 1) * LANE]
                    * jnp.where(is_q, HEAD_DIM ** -0.5 * LOG2_E, 1.0))
            nat_ref[c] = slab.astype(BF16)
            scr_ref[c] = slab
            for r in range(n_cls):
                cm_ref[c, r] = scr_ref[c, pl.ds(r, rows, stride=n_cls), :]


def _attn_proj(h, w_all, layer, batch, seq, tm=1024, tn=1024):
    m, d = h.shape
    tiles_per_seq = seq // tm
    slabs = tn // LANE
    return pl.pallas_call(
        _attn_proj_kernel,
        grid=(ATTN_SLABS * LANE // tn, m // tm),
        in_specs=[
            pl.BlockSpec((tm, d), lambda j, i: (i, 0)),
            pl.BlockSpec((None, d, tn), lambda j, i: (layer, 0, COL_QA * LANE // tn + j)),
        ],
        out_specs=[
            pl.BlockSpec((slabs, tm, LANE), lambda j, i: (j, i, 0)),
            pl.BlockSpec((slabs, None, N_CLASSES, tm // N_CLASSES, LANE),
                         lambda j, i: (j, i // tiles_per_seq, 0, i % tiles_per_seq, 0)),
        ],
        out_shape=[jax.ShapeDtypeStruct((ATTN_SLABS, m, LANE), BF16),
                   jax.ShapeDtypeStruct((ATTN_SLABS, batch, N_CLASSES, seq // N_CLASSES, LANE), F32)],
        scratch_shapes=[pltpu.VMEM((slabs, tm, LANE), F32)],
        compiler_params=pltpu.CompilerParams(
            dimension_semantics=("parallel", "parallel"), vmem_limit_bytes=VMEM_LIMIT),
        name="attn_proj",
    )(h, w_all)


def _conv_proj_kernel(h_ref, wu_ref, wb_ref, wc_ref, wz_ref, cw_ref, cb_ref, o_ref, halo_ref,
                      *, tiles_per_seq):
    i = pl.program_id(1)
    tm = h_ref.shape[0]

    @pl.when(i == 0)
    def _():
        halo_ref[...] = jnp.zeros(halo_ref.shape, F32)

    h = h_ref[...]
    cu = _dot(h, wc_ref[...].astype(BF16)) * _dot(h, wu_ref[...].astype(BF16))
    halo = jnp.where((i % tiles_per_seq) == 0, 0.0, halo_ref[...])
    halo_ref[...] = cu[tm - 8:, :]
    row = lax.broadcasted_iota(jnp.int32, (8, cu.shape[1]), 0)
    back1 = pltpu.roll(cu, 1, 0)
    back2 = pltpu.roll(cu, 2, 0)
    head1 = jnp.where(row == 0, halo[7:8, :], back1[:8])
    head2 = jnp.where(row == 0, halo[6:7, :], jnp.where(row == 1, halo[7:8, :], back2[:8]))
    back1 = jnp.concatenate([head1, back1[8:]], axis=0)
    back2 = jnp.concatenate([head2, back2[8:]], axis=0)
    conv = (cb_ref[...] + cw_ref[2:3, :] * cu + cw_ref[0:1, :] * back2 + cw_ref[1:2, :] * back1)
    hz = 0.5 * _dot(h, wz_ref[...].astype(BF16))
    gated = (conv * (_dot(h, wb_ref[...].astype(BF16)) * hz)) * (1.0 + jnp.tanh(hz))
    o_ref[...] = gated.astype(BF16)


def _conv_proj(h, w_all, conv_w, conv_b, layer, seq, tm=1024, tc=256):
    m, d = h.shape

    def wspec(col):
        return pl.BlockSpec((None, d, tc), lambda jc, i: (layer, 0, col * LANE // tc + jc))

    return pl.pallas_call(
        functools.partial(_conv_proj_kernel, tiles_per_seq=seq // tm),
        grid=(WIDTH // tc, m // tm),
        in_specs=[pl.BlockSpec((tm, d), lambda jc, i: (i, 0)),
                  wspec(COL_UC), wspec(COL_BC), wspec(COL_CC), wspec(COL_ZC),
                  pl.BlockSpec((CONV_K, tc), lambda jc, i: (0, jc)),
                  pl.BlockSpec((1, tc), lambda jc, i: (0, jc))],
        out_specs=pl.BlockSpec((tm, tc), lambda jc, i: (i, jc)),
        out_shape=jax.ShapeDtypeStruct((m, WIDTH), BF16),
        scratch_shapes=[pltpu.VMEM((8, tc), F32)],
        compiler_params=pltpu.CompilerParams(
            dimension_semantics=("parallel", "arbitrary"), vmem_limit_bytes=VMEM_LIMIT),
        name="conv_proj",
    )(h, w_all, w_all, w_all, w_all, conv_w, conv_b.reshape(1, WIDTH))


def _attn_kernel(qn_ref, kn_ref, vn_ref, zn_ref, qc_ref, kc_ref, vc_ref, o_ref,
                 o4, lse4, o16, lse16):
    seq = qn_ref.shape[1]
    t = ATTN_TILE
    e = HEAD_DIM
    n_cls = qc_ref.shape[0]
    row = lax.broadcasted_iota(jnp.int32, (t, t), 0)
    col = lax.broadcasted_iota(jnp.int32, (t, t), 1)

    def bdot(a, b, contract_b):
        return lax.dot_general(a, b, (((2,), (contract_b,)), ((0,), (0,))),
                               preferred_element_type=F32)

    def partials(q, k_cur, k_prev, v_cur, v_prev, cur_ok, prev_ok):
        g = q.shape[0]
        keys = jnp.concatenate([k_prev, k_cur], axis=1)
        vals = jnp.concatenate([v_prev, v_cur], axis=1)
        ok = jnp.concatenate([jnp.broadcast_to(prev_ok, (g, t, t)),
                              jnp.broadcast_to(cur_ok, (g, t, t))], axis=2)
        s = jnp.where(ok, bdot(q, keys, 2), NEG_BIG)
        m = jnp.max(s, axis=-1, keepdims=True)
        p = jnp.exp2(s - m).astype(BF16)
        acc_l = bdot(p, jnp.concatenate([vals, jnp.ones(vals.shape, BF16)], axis=2), 1)
        return acc_l[:, :, :e], m, acc_l[:, :, e:]

    def previous_tiles(x):
        return jnp.concatenate([x[:, :1], x[:, :-1]], axis=1)

    def dilated_pass(dil, o_s, lse_s):
        g = ATTN_BATCH[dil]
        step = dil // n_cls
        tiles = seq // (dil * t)
        classes = g // tiles
        slot = lax.broadcasted_iota(jnp.int32, (g, t, t), 0)
        cur_ok = col <= row
        prev_ok = jnp.logical_and(col >= row, slot % tiles > 0)

        def class_rows(ref, r):
            if step == 1:
                return ref[r]
            return ref[r % n_cls, pl.ds(r // n_cls, tiles * t, stride=step), :]

        def gather(ref, r0):
            return jnp.stack([class_rows(ref, r0 + c).astype(BF16).reshape(tiles, t, e)
                              for c in range(classes)], axis=0)

        def body(i, carry):
            r0 = i * classes
            k, v = gather(kc_ref, r0), gather(vc_ref, r0)
            acc, m, l = partials(gather(qc_ref, r0).reshape(g, t, e),
                                 k.reshape(g, t, e), previous_tiles(k).reshape(g, t, e),
                                 v.reshape(g, t, e), previous_tiles(v).reshape(g, t, e),
                                 cur_ok, prev_ok)
            out = acc / l
            lse = m + jnp.log2(l)
            for c in range(classes):
                dst = pl.ds(r0 + c, tiles * t, stride=dil)
                part = lambda x: x[c * tiles:(c + 1) * tiles].reshape(tiles * t, e)
                o_s[dst, :] = part(out)
                lse_s[dst, :] = part(lse)
            return carry

        lax.fori_loop(0, dil // classes, body, 0)

    dilated_pass(4, o4, lse4)
    dilated_pass(16, o16, lse16)

    g = ATTN_BATCH[1]
    slot = lax.broadcasted_iota(jnp.int32, (g, t, t), 0)
    for i in range(seq // (g * t)):
        start = i * g * t
        rows = slice(start, start + g * t)
        if i == 0:
            with_prev = lambda ref: jnp.concatenate([ref[0, :t], ref[0, rows]], axis=0)
            prev_ok = jnp.logical_and(col >= row, slot > 0)
        else:
            with_prev = lambda ref: ref[0, start - t:start + g * t]
            prev_ok = col >= row
        k_all, v_all = with_prev(kn_ref), with_prev(vn_ref)
        acc1, m1, l1 = partials(qn_ref[0, rows].reshape(g, t, e),
                                k_all[t:].reshape(g, t, e), k_all[:g * t].reshape(g, t, e),
                                v_all[t:].reshape(g, t, e), v_all[:g * t].reshape(g, t, e),
                                col <= row, prev_ok)
        load = lambda ref: ref[rows, :].reshape(g, t, e)
        mm4, mm16 = load(lse4), load(lse16)
        m = jnp.maximum(jnp.maximum(m1, mm4), mm16)
        w1 = jnp.exp2(m1 - m)
        w4 = jnp.exp2(mm4 - m)
        w16 = jnp.exp2(mm16 - m)
        num = w1 * acc1 + w4 * load(o4) + w16 * load(o16)
        den = w1 * l1 + w4 + w16
        z = zn_ref[0, rows].astype(F32).reshape(g, t, e)
        o_ref[rows, :] = (num / den * _silu(z)).astype(BF16).reshape(g * t, e)


def _attention(nat, cm, batch, seq):
    m = nat.shape[1]

    def spec(base):
        return pl.BlockSpec((1, seq, LANE), lambda b, h: (base + h, b, 0))

    def cm_spec(base):
        return pl.BlockSpec((None, None, N_CLASSES, seq // N_CLASSES, LANE),
                            lambda b, h: (base + h, b, 0, 0, 0))

    return pl.pallas_call(
        _attn_kernel,
        grid=(batch, N_HEADS),
        in_specs=[spec(SLAB_QA), spec(SLAB_KA), spec(SLAB_VA), spec(SLAB_ZA),
                  cm_spec(SLAB_QA), cm_spec(SLAB_KA), cm_spec(SLAB_VA)],
        out_specs=pl.BlockSpec((seq, LANE), lambda b, h: (b, h)),
        out_shape=jax.ShapeDtypeStruct((m, WIDTH), BF16),
        scratch_shapes=[pltpu.VMEM((seq, LANE), F32)] * 4,
        compiler_params=pltpu.CompilerParams(
            dimension_semantics=("parallel", "parallel"), vmem_limit_bytes=VMEM_LIMIT),
        name="dilated_attn",
    )(nat, nat, nat, nat, cm, cm, cm)


def _ret_kernel(lg_ref, q_ref, k_ref, v_ref, z_ref, cos_ref, sin_ref, o_ref):
    seq = q_ref.shape[1]
    c = RET_CHUNK
    lg = lg_ref[pl.program_id(1)]
    row = lax.broadcasted_iota(jnp.int32, (c, c), 0).astype(F32)
    col = lax.broadcasted_iota(jnp.int32, (c, c), 1).astype(F32)
    rel = row - col
    key_scale = HEAD_DIM ** -0.5
    decay_mask = jnp.where(rel >= 0, jnp.exp(jnp.maximum(rel, 0.0) * lg), 0.0) * key_scale
    q_decay = jnp.exp((row + 1.0) * lg)
    k_decay = jnp.exp((c - 1.0 - row) * lg) * key_scale
    chunk_decay = jnp.exp(jnp.full((c, c), float(c), F32) * lg)

    g = RET_BATCH
    e = HEAD_DIM

    def bdot(a, b, contract_a, contract_b):
        return lax.dot_general(a, b, (((contract_a,), (contract_b,)), ((0,), (0,))),
                               preferred_element_type=F32)

    def body(i, state):
        sl = pl.ds(pl.multiple_of(i * (g * c), g * c), g * c)
        cos, sin = cos_ref[sl, :], sin_ref[sl, :]
        rotate = lambda x: x * cos + pltpu.roll(x, e // 2, 1) * sin
        q = rotate(q_ref[0, sl, :].astype(F32)).reshape(g, c, e)
        k = rotate(k_ref[0, sl, :].astype(F32)).reshape(g, c, e)
        v = v_ref[0, sl, :].reshape(g, c, e)
        inner = bdot(q.astype(BF16), k.astype(BF16), 2, 2) * decay_mask
        kv = bdot((k * k_decay).astype(BF16), v, 1, 1)
        states = []
        for j in range(g):
            states.append(state)
            state = state * chunk_decay + kv[j]
        lhs = jnp.concatenate([inner.astype(BF16), (q * q_decay).astype(BF16)], axis=2)
        rhs = jnp.concatenate([v, jnp.stack(states).astype(BF16)], axis=1)
        o = bdot(lhs, rhs, 2, 1)
        o = o * lax.rsqrt(jnp.mean(o * o, axis=-1, keepdims=True) + NORM_EPS)
        z = z_ref[0, sl, :].astype(F32).reshape(g, c, e)
        o_ref[sl, :] = (o * _silu(z)).astype(BF16).reshape(g * c, e)
        return state

    lax.fori_loop(0, seq // (g * c), body, jnp.zeros((e, e), F32))


def _retention(proj, log_gamma, cos2, sin2, batch, seq):
    m = proj.shape[1]

    def spec(base):
        return pl.BlockSpec((1, seq, LANE), lambda b, h, lg: (base + h, b, 0))

    table = pl.BlockSpec((seq, LANE), lambda b, h, lg: (0, 0))
    return pl.pallas_call(
        _ret_kernel,
        grid_spec=pltpu.PrefetchScalarGridSpec(
            num_scalar_prefetch=1,
            grid=(batch, N_HEADS),
            in_specs=[spec(SLAB_QR), spec(SLAB_KR), spec(SLAB_VR), spec(SLAB_ZR), table, table],
            out_specs=pl.BlockSpec((seq, LANE), lambda b, h, lg: (b, h)),
        ),
        out_shape=jax.ShapeDtypeStruct((m, WIDTH), BF16),
        compiler_params=pltpu.CompilerParams(
            dimension_semantics=("parallel", "parallel"), vmem_limit_bytes=VMEM_LIMIT),
        name="retention",
    )(log_gamma, proj, proj, proj, proj, cos2, sin2)


def _merge_kernel(oa_ref, oc_ref, or_ref, ga_ref, gc_ref, gr_ref, wa_ref, wc_ref, wr_ref, o_ref):
    ya = _dot(oa_ref[...], wa_ref[...].astype(BF16))
    yc = _dot(oc_ref[...], wc_ref[...].astype(BF16))
    yr = _dot(or_ref[...], wr_ref[...].astype(BF16))
    for s in range(ga_ref.shape[0]):
        lanes = slice(s * LANE, (s + 1) * LANE)
        merged = (_sigmoid(ga_ref[s].astype(F32)) * ya[:, lanes]
                  + _sigmoid(gc_ref[s].astype(F32)) * yc[:, lanes]
                  + _sigmoid(gr_ref[s].astype(F32)) * yr[:, lanes])
        o_ref[:, lanes] = merged.astype(BF16)


def _merge(proj, oa, oc, orr, wa, wc, wr, layer, tm=1024, tn=512):
    m = proj.shape[1]
    gs = tn // LANE

    def gate(base):
        return pl.BlockSpec((gs, tm, LANE), lambda n, i: (base // gs + n, i, 0))

    wspec = pl.BlockSpec((None, WIDTH, tn), lambda n, i: (layer, 0, n))
    own = pl.BlockSpec((tm, WIDTH), lambda n, i: (i, 0))
    return pl.pallas_call(
        _merge_kernel,
        grid=(D_MODEL // tn, m // tm),
        in_specs=[own, own, own, gate(SLAB_GA), gate(SLAB_GC), gate(SLAB_GR), wspec, wspec, wspec],
        out_specs=pl.BlockSpec((tm, tn), lambda n, i: (i, n)),
        out_shape=jax.ShapeDtypeStruct((m, D_MODEL), BF16),
        compiler_params=pltpu.CompilerParams(
            dimension_semantics=("parallel", "parallel"), vmem_limit_bytes=VMEM_LIMIT),
        name="branch_merge",
    )(oa, oc, orr, proj, proj, proj, wa, wc, wr)


def _out_proj_kernel(m_ref, w_ref, x_ref, g_ref, *rest):
    y = _dot(m_ref[...], w_ref[...].astype(BF16))
    ms = jnp.mean(y * y, axis=-1, keepdims=True)
    x_new = x_ref[...] + y * lax.rsqrt(ms + NORM_EPS) * g_ref[...]
    if len(rest) == 1:
        (o_ref,) = rest
    else:
        next_g_ref, o_ref, h_ref = rest
        ms = jnp.mean(x_new * x_new, axis=-1, keepdims=True)
        h_ref[...] = (x_new * lax.rsqrt(ms + NORM_EPS) * next_g_ref[...]).astype(BF16)
    o_ref[...] = x_new


def _out_proj(merged, w_all, layer, x2, g, next_g=None, tm=512):
    m, d = x2.shape
    rows = pl.BlockSpec((tm, d), lambda i: (i, 0))
    gain = pl.BlockSpec((1, d), lambda i: (0, 0))
    more = next_g is not None
    out = pl.pallas_call(
        _out_proj_kernel,
        grid=(m // tm,),
        in_specs=[rows,
                  pl.BlockSpec((None, d, d), lambda i: (layer, 0, 0), pipeline_mode=pl.Buffered(1)),
                  rows, gain] + [gain] * more,
        out_specs=[rows] + [rows] * more,
        out_shape=[jax.ShapeDtypeStruct((m, d), F32)] + [jax.ShapeDtypeStruct((m, d), BF16)] * more,
        compiler_params=pltpu.CompilerParams(
            dimension_semantics=("parallel",), vmem_limit_bytes=VMEM_LIMIT),
        name="out_proj",
    )(merged, w_all, x2, g.reshape(1, d), *([next_g.reshape(1, d)] if more else []))
    return (out[0], out[1]) if more else (out[0], None)


def _rotary_tables(seq):
    half = HEAD_DIM // 2
    inv_freq = ROPE_BASE ** (-jnp.arange(half, dtype=F32) / half)
    ang = jnp.arange(seq, dtype=F32)[:, None] * inv_freq[None, :]
    cos, sin = jnp.cos(ang), jnp.sin(ang)
    return jnp.concatenate([cos, cos], axis=-1), jnp.concatenate([-sin, sin], axis=-1)


def kernel(x, pre_norm_g, post_norm_g, w_in, conv_w, conv_b, w_branch_a, w_branch_c, w_branch_r, w_out):
    batch, seq, d = x.shape
    assert d == D_MODEL and w_in.shape[-1] == N_IN
    assert seq % (max(DILATIONS) * ATTN_TILE) == 0 and seq % RET_CHUNK == 0
    cos2, sin2 = _rotary_tables(seq)
    log_gamma = jnp.log1p(-jnp.exp2(-5.0 - jnp.arange(N_HEADS, dtype=F32)))
    x2 = x.reshape(batch * seq, d)
    depth = w_in.shape[0]
    h = _pre_norm(x2, pre_norm_g[0])
    for layer in range(depth):
        attn_nat, attn_cm = _attn_proj(h, w_in, layer, batch, seq)
        oa = _attention(attn_nat, attn_cm, batch, seq)
        proj = _in_proj(h, w_in, layer, COL_QR, N_IN // LANE - COL_QR)
        orr = _retention(proj, log_gamma, cos2, sin2, batch, seq)
        oc = _conv_proj(h, w_in, conv_w[layer], conv_b[layer], layer, seq)
        merged = _merge(proj, oa, oc, orr, w_branch_a, w_branch_c, w_branch_r, layer)
        next_g = pre_norm_g[layer + 1] if layer + 1 < depth else None
        x2, h = _out_proj(merged, w_out, layer, x2, post_norm_g[layer], next_g)
    return x2.reshape(batch, seq, d)
```

```python
import functools
import math

import jax
import jax.numpy as jnp
from jax import lax
from jax.experimental import pallas as pl
from jax.experimental.pallas import tpu as pltpu

F32 = jnp.float32
BF16 = jnp.bfloat16

LANE = 128
D_MODEL = 2048
HEAD_DIM = 128
N_HEADS = 12
WIDTH = N_HEADS * HEAD_DIM
N_IN = 12 * WIDTH + 3 * D_MODEL
DILATIONS = (1, 4, 16)
ATTN_TILE = 128
ATTN_BATCH = {1: 16, 4: 16, 16: 16}
LOG2_E = math.log2(math.e)
RET_CHUNK = 128
RET_BATCH = 8
CONV_K = 3
ROPE_BASE = 10000.0
NORM_EPS = 1e-6
NEG_BIG = -1e30

COL_QA, COL_KA, COL_VA, COL_ZA = 0, 12, 24, 36
COL_UC, COL_BC, COL_CC, COL_ZC = 48, 60, 72, 84
COL_QR, COL_KR, COL_VR, COL_ZR = 96, 108, 120, 132
COL_GA, COL_GC, COL_GR = 144, 160, 176
ATTN_SLABS = COL_UC - COL_QA
SLAB_QA, SLAB_KA, SLAB_VA, SLAB_ZA = COL_QA, COL_KA, COL_VA, COL_ZA
SLAB_QR, SLAB_KR, SLAB_VR, SLAB_ZR = (c - COL_QR for c in (COL_QR, COL_KR, COL_VR, COL_ZR))
SLAB_GA, SLAB_GC, SLAB_GR = (c - COL_QR for c in (COL_GA, COL_GC, COL_GR))
N_CLASSES = 4

VMEM_LIMIT = 56 * 1024 * 1024


def _sigmoid(x):
    return 0.5 * jnp.tanh(0.5 * x) + 0.5


def _silu(x):
    return x * _sigmoid(x)


def _dot(a, b):
    return jnp.dot(a, b, preferred_element_type=F32)


def _pre_norm_kernel(x_ref, g_ref, o_ref):
    xf = x_ref[...]
    ms = jnp.mean(xf * xf, axis=-1, keepdims=True)
    o_ref[...] = (xf * lax.rsqrt(ms + NORM_EPS) * g_ref[...]).astype(BF16)


def _pre_norm(x2, g, tm=512):
    m, d = x2.shape
    return pl.pallas_call(
        _pre_norm_kernel,
        grid=(m // tm,),
        in_specs=[pl.BlockSpec((tm, d), lambda i: (i, 0)), pl.BlockSpec((1, d), lambda i: (0, 0))],
        out_specs=pl.BlockSpec((tm, d), lambda i: (i, 0)),
        out_shape=jax.ShapeDtypeStruct((m, d), BF16),
        compiler_params=pltpu.CompilerParams(
            dimension_semantics=("parallel",), vmem_limit_bytes=VMEM_LIMIT),
        name="pre_norm",
    )(x2, g.reshape(1, d))


ATTN_SEGMENTS = ((SLAB_QA, SLAB_KA, "qscale"), (SLAB_KA, SLAB_ZA, "plain"), (SLAB_ZA, ATTN_SLABS, "silu"))
REST_SEGMENTS = ((SLAB_QR, SLAB_VR, "rotary"), (SLAB_VR, SLAB_ZR, "plain"), (SLAB_ZR, SLAB_GA, "silu"),
                 (SLAB_GA, N_IN // LANE - COL_QR, "sigmoid"))


def _tile_modes(segments, slabs_per_tile):
    per_slab = [mode for start, end, mode in segments for _ in range(end - start)]
    return tuple(tuple(per_slab[i:i + slabs_per_tile]) for i in range(0, len(per_slab), slabs_per_tile))


def _activate(slab, mode, cos_ref=None, sin_ref=None):
    if mode == "qscale":
        return slab * (HEAD_DIM ** -0.5 * LOG2_E)
    if mode == "silu":
        return _silu(slab)
    if mode == "sigmoid":
        return _sigmoid(slab)
    if mode == "rotary":
        return slab * cos_ref[...] + pltpu.roll(slab, HEAD_DIM // 2, 1) * sin_ref[...]
    return slab


def _project_tile(tile_modes, j, h_ref, w_ref, emit):
    def run(pattern):
        for c0 in range(0, len(pattern), 2):
            acc = _dot(h_ref[...], w_ref[:, c0 * LANE:(c0 + 2) * LANE].astype(BF16))
            for c in (c0, c0 + 1):
                emit(c, acc[:, (c - c0) * LANE:(c - c0 + 1) * LANE], pattern[c])

    for pattern in dict.fromkeys(tile_modes):
        hit = functools.reduce(jnp.logical_or, [j == i for i, p in enumerate(tile_modes) if p == pattern])
        pl.when(hit)(functools.partial(run, pattern))


def _in_proj_kernel(h_ref, w_ref, cos_ref, sin_ref, o_ref, *, tile_modes):
    def emit(c, slab, mode):
        o_ref[c] = _activate(slab, mode, cos_ref, sin_ref).astype(BF16)

    _project_tile(tile_modes, pl.program_id(0), h_ref, w_ref, emit)


def _in_proj(h, w_all, layer, cos2, sin2, seq, tm=1024, tn=1024):
    m, d = h.shape
    tile_modes = _tile_modes(REST_SEGMENTS, tn // LANE)
    tile0 = COL_QR * LANE // tn
    tiles_per_seq = seq // tm
    table = pl.BlockSpec((tm, LANE), lambda j, i: (i % tiles_per_seq, 0))
    return pl.pallas_call(
        functools.partial(_in_proj_kernel, tile_modes=tile_modes),
        grid=(len(tile_modes), m // tm),
        in_specs=[
            pl.BlockSpec((tm, d), lambda j, i: (i, 0)),
            pl.BlockSpec((None, d, tn), lambda j, i: (layer, 0, tile0 + j)),
            table, table,
        ],
        out_specs=pl.BlockSpec((tn // LANE, tm, LANE), lambda j, i: (j, i, 0)),
        out_shape=jax.ShapeDtypeStruct((len(tile_modes) * tn // LANE, m, LANE), BF16),
        compiler_params=pltpu.CompilerParams(
            dimension_semantics=("parallel", "parallel"), vmem_limit_bytes=VMEM_LIMIT),
        name="in_proj",
    )(h, w_all, cos2, sin2)


def _attn_proj_kernel(h_ref, w_ref, nat_ref, cm_ref, scr_ref, *, tile_modes):
    n_cls, rows = cm_ref.shape[1], cm_ref.shape[2]

    def emit(c, slab, mode):
        slab = _activate(slab, mode)
        nat_ref[c] = slab.astype(BF16)
        scr_ref[c] = slab
        for r in range(n_cls):
            cm_ref[c, r] = scr_ref[c, pl.ds(r, rows, stride=n_cls), :]

    _project_tile(tile_modes, pl.program_id(0), h_ref, w_ref, emit)


def _attn_proj(h, w_all, layer, batch, seq, tm=1024, tn=1024):
    m, d = h.shape
    tiles_per_seq = seq // tm
    slabs = tn // LANE
    tile_modes = _tile_modes(ATTN_SEGMENTS, slabs)
    return pl.pallas_call(
        functools.partial(_attn_proj_kernel, tile_modes=tile_modes),
        grid=(len(tile_modes), m // tm),
        in_specs=[
            pl.BlockSpec((tm, d), lambda j, i: (i, 0)),
            pl.BlockSpec((None, d, tn), lambda j, i: (layer, 0, COL_QA * LANE // tn + j)),
        ],
        out_specs=[
            pl.BlockSpec((slabs, tm, LANE), lambda j, i: (j, i, 0)),
            pl.BlockSpec((slabs, None, N_CLASSES, tm // N_CLASSES, LANE),
                         lambda j, i: (j, i // tiles_per_seq, 0, i % tiles_per_seq, 0)),
        ],
        out_shape=[jax.ShapeDtypeStruct((ATTN_SLABS, m, LANE), BF16),
                   jax.ShapeDtypeStruct((ATTN_SLABS, batch, N_CLASSES, seq // N_CLASSES, LANE), F32)],
        scratch_shapes=[pltpu.VMEM((slabs, tm, LANE), F32)],
        compiler_params=pltpu.CompilerParams(
            dimension_semantics=("parallel", "parallel"), vmem_limit_bytes=VMEM_LIMIT),
        name="attn_proj",
    )(h, w_all)


def _conv_proj_kernel(h_ref, wu_ref, wb_ref, wc_ref, wz_ref, cw_ref, cb_ref, o_ref, halo_ref,
                      *, tiles_per_seq):
    i = pl.program_id(1)
    tm = h_ref.shape[0]

    @pl.when(i == 0)
    def _():
        halo_ref[...] = jnp.zeros(halo_ref.shape, F32)

    h = h_ref[...]
    cu = _dot(h, wc_ref[...].astype(BF16)) * _dot(h, wu_ref[...].astype(BF16))
    halo = jnp.where((i % tiles_per_seq) == 0, 0.0, halo_ref[...])
    halo_ref[...] = cu[tm - 8:, :]
    row = lax.broadcasted_iota(jnp.int32, (8, cu.shape[1]), 0)
    back1 = pltpu.roll(cu, 1, 0)
    back2 = pltpu.roll(cu, 2, 0)
    head1 = jnp.where(row == 0, halo[7:8, :], back1[:8])
    head2 = jnp.where(row == 0, halo[6:7, :], jnp.where(row == 1, halo[7:8, :], back2[:8]))
    back1 = jnp.concatenate([head1, back1[8:]], axis=0)
    back2 = jnp.concatenate([head2, back2[8:]], axis=0)
    conv = (cb_ref[...] + cw_ref[2:3, :] * cu + cw_ref[0:1, :] * back2 + cw_ref[1:2, :] * back1)
    hz = 0.5 * _dot(h, wz_ref[...].astype(BF16))
    gated = (conv * (_dot(h, wb_ref[...].astype(BF16)) * hz)) * (1.0 + jnp.tanh(hz))
    o_ref[...] = gated.astype(BF16)


def _conv_proj(h, w_all, conv_w, conv_b, layer, seq, tm=1024, tc=256):
    m, d = h.shape

    def wspec(col):
        return pl.BlockSpec((None, d, tc), lambda jc, i: (layer, 0, col * LANE // tc + jc))

    return pl.pallas_call(
        functools.partial(_conv_proj_kernel, tiles_per_seq=seq // tm),
        grid=(WIDTH // tc, m // tm),
        in_specs=[pl.BlockSpec((tm, d), lambda jc, i: (i, 0)),
                  wspec(COL_UC), wspec(COL_BC), wspec(COL_CC), wspec(COL_ZC),
                  pl.BlockSpec((CONV_K, tc), lambda jc, i: (0, jc)),
                  pl.BlockSpec((1, tc), lambda jc, i: (0, jc))],
        out_specs=pl.BlockSpec((tm, tc), lambda jc, i: (i, jc)),
        out_shape=jax.ShapeDtypeStruct((m, WIDTH), BF16),
        scratch_shapes=[pltpu.VMEM((8, tc), F32)],
        compiler_params=pltpu.CompilerParams(
            dimension_semantics=("parallel", "arbitrary"), vmem_limit_bytes=VMEM_LIMIT),
        name="conv_proj",
    )(h, w_all, w_all, w_all, w_all, conv_w, conv_b.reshape(1, WIDTH))


def _attn_kernel(qn_ref, kn_ref, vn_ref, zn_ref, qc_ref, kc_ref, vc_ref, o_ref,
                 o4, lse4, o16, lse16):
    seq = qn_ref.shape[1]
    t = ATTN_TILE
    e = HEAD_DIM
    n_cls = qc_ref.shape[0]
    row = lax.broadcasted_iota(jnp.int32, (t, t), 0)
    col = lax.broadcasted_iota(jnp.int32, (t, t), 1)

    def bdot(a, b, contract_b):
        return lax.dot_general(a, b, (((2,), (contract_b,)), ((0,), (0,))),
                               preferred_element_type=F32)

    def partials(q, k_cur, k_prev, v_cur, v_prev, cur_ok, prev_ok):
        g = q.shape[0]
        keys = jnp.concatenate([k_prev, k_cur], axis=1)
        vals = jnp.concatenate([v_prev, v_cur], axis=1)
        ok = jnp.concatenate([jnp.broadcast_to(prev_ok, (g, t, t)),
                              jnp.broadcast_to(cur_ok, (g, t, t))], axis=2)
        s = jnp.where(ok, bdot(q, keys, 2), NEG_BIG)
        m = jnp.max(s, axis=-1, keepdims=True)
        p = jnp.exp2(s - m).astype(BF16)
        acc_l = bdot(p, jnp.concatenate([vals, jnp.ones(vals.shape, BF16)], axis=2), 1)
        return acc_l[:, :, :e], m, acc_l[:, :, e:]

    def previous_tiles(x):
        return jnp.concatenate([x[:, :1], x[:, :-1]], axis=1)

    def dilated_pass(dil, o_s, lse_s):
        g = ATTN_BATCH[dil]
        step = dil // n_cls
        tiles = seq // (dil * t)
        classes = g // tiles
        slot = lax.broadcasted_iota(jnp.int32, (g, t, t), 0)
        cur_ok = col <= row
        prev_ok = jnp.logical_and(col >= row, slot % tiles > 0)

        def class_rows(ref, r):
            if step == 1:
                return ref[r]
            return ref[r % n_cls, pl.ds(r // n_cls, tiles * t, stride=step), :]

        def gather(ref, r0):
            return jnp.stack([class_rows(ref, r0 + c).astype(BF16).reshape(tiles, t, e)
                              for c in range(classes)], axis=0)

        def body(i, carry):
            r0 = i * classes
            k, v = gather(kc_ref, r0), gather(vc_ref, r0)
            acc, m, l = partials(gather(qc_ref, r0).reshape(g, t, e),
                                 k.reshape(g, t, e), previous_tiles(k).reshape(g, t, e),
                                 v.reshape(g, t, e), previous_tiles(v).reshape(g, t, e),
                                 cur_ok, prev_ok)
            out = acc / l
            lse = m + jnp.log2(l)
            for c in range(classes):
                dst = pl.ds(r0 + c, tiles * t, stride=dil)
                part = lambda x: x[c * tiles:(c + 1) * tiles].reshape(tiles * t, e)
                o_s[dst, :] = part(out)
                lse_s[dst, :] = part(lse)
            return carry

        lax.fori_loop(0, dil // classes, body, 0)

    dilated_pass(4, o4, lse4)
    dilated_pass(16, o16, lse16)

    g = ATTN_BATCH[1]
    slot = lax.broadcasted_iota(jnp.int32, (g, t, t), 0)
    for i in range(seq // (g * t)):
        start = i * g * t
        rows = slice(start, start + g * t)
        if i == 0:
            with_prev = lambda ref: jnp.concatenate([ref[0, :t], ref[0, rows]], axis=0)
            prev_ok = jnp.logical_and(col >= row, slot > 0)
        else:
            with_prev = lambda ref: ref[0, start - t:start + g * t]
            prev_ok = col >= row
        k_all, v_all = with_prev(kn_ref), with_prev(vn_ref)
        acc1, m1, l1 = partials(qn_ref[0, rows].reshape(g, t, e),
                                k_all[t:].reshape(g, t, e), k_all[:g * t].reshape(g, t, e),
                                v_all[t:].reshape(g, t, e), v_all[:g * t].reshape(g, t, e),
                                col <= row, prev_ok)
        load = lambda ref: ref[rows, :].reshape(g, t, e)
        mm4, mm16 = load(lse4), load(lse16)
        m = jnp.maximum(jnp.maximum(m1, mm4), mm16)
        w1 = jnp.exp2(m1 - m)
        w4 = jnp.exp2(mm4 - m)
        w16 = jnp.exp2(mm16 - m)
        num = w1 * acc1 + w4 * load(o4) + w16 * load(o16)
        den = w1 * l1 + w4 + w16
        gate = zn_ref[0, rows].astype(F32).reshape(g, t, e)
        o_ref[rows, :] = (num / den * gate).astype(BF16).reshape(g * t, e)


def _attention(nat, cm, batch, seq):
    m = nat.shape[1]

    def spec(base):
        return pl.BlockSpec((1, seq, LANE), lambda b, h: (base + h, b, 0))

    def cm_spec(base):
        return pl.BlockSpec((None, None, N_CLASSES, seq // N_CLASSES, LANE),
                            lambda b, h: (base + h, b, 0, 0, 0))

    return pl.pallas_call(
        _attn_kernel,
        grid=(batch, N_HEADS),
        in_specs=[spec(SLAB_QA), spec(SLAB_KA), spec(SLAB_VA), spec(SLAB_ZA),
                  cm_spec(SLAB_QA), cm_spec(SLAB_KA), cm_spec(SLAB_VA)],
        out_specs=pl.BlockSpec((seq, LANE), lambda b, h: (b, h)),
        out_shape=jax.ShapeDtypeStruct((m, WIDTH), BF16),
        scratch_shapes=[pltpu.VMEM((seq, LANE), F32)] * 4,
        compiler_params=pltpu.CompilerParams(
            dimension_semantics=("parallel", "parallel"), vmem_limit_bytes=VMEM_LIMIT),
        name="dilated_attn",
    )(nat, nat, nat, nat, cm, cm, cm)


def _ret_kernel(lg_ref, q_ref, k_ref, v_ref, z_ref, o_ref):
    seq = q_ref.shape[1]
    c = RET_CHUNK
    lg = lg_ref[pl.program_id(1)]
    row = lax.broadcasted_iota(jnp.int32, (c, c), 0).astype(F32)
    col = lax.broadcasted_iota(jnp.int32, (c, c), 1).astype(F32)
    rel = row - col
    key_scale = HEAD_DIM ** -0.5
    decay_mask = jnp.where(rel >= 0, jnp.exp(jnp.maximum(rel, 0.0) * lg), 0.0) * key_scale
    q_decay = jnp.exp((row + 1.0) * lg)
    k_decay = jnp.exp((c - 1.0 - row) * lg) * key_scale
    chunk_decay = jnp.exp(jnp.full((c, c), float(c), F32) * lg)

    g = RET_BATCH
    e = HEAD_DIM

    def bdot(a, b, contract_a, contract_b):
        return lax.dot_general(a, b, (((contract_a,), (contract_b,)), ((0,), (0,))),
                               preferred_element_type=F32)

    def body(i, state):
        sl = pl.ds(pl.multiple_of(i * (g * c), g * c), g * c)
        q = q_ref[0, sl, :].reshape(g, c, e)
        k = k_ref[0, sl, :].reshape(g, c, e)
        v = v_ref[0, sl, :].reshape(g, c, e)
        inner = bdot(q, k, 2, 2) * decay_mask
        kv = bdot((k.astype(F32) * k_decay).astype(BF16), v, 1, 1)
        states = []
        for j in range(g):
            states.append(state)
            state = state * chunk_decay + kv[j]
        lhs = jnp.concatenate([inner.astype(BF16), (q.astype(F32) * q_decay).astype(BF16)], axis=2)
        rhs = jnp.concatenate([v, jnp.stack(states).astype(BF16)], axis=1)
        o = bdot(lhs, rhs, 2, 1)
        o = o * lax.rsqrt(jnp.mean(o * o, axis=-1, keepdims=True) + NORM_EPS)
        gate = z_ref[0, sl, :].astype(F32).reshape(g, c, e)
        o_ref[sl, :] = (o * gate).astype(BF16).reshape(g * c, e)
        return state

    lax.fori_loop(0, seq // (g * c), body, jnp.zeros((e, e), F32))


def _retention(proj, log_gamma, batch, seq):
    m = proj.shape[1]

    def spec(base):
        return pl.BlockSpec((1, seq, LANE), lambda b, h, lg: (base + h, b, 0))

    return pl.pallas_call(
        _ret_kernel,
        grid_spec=pltpu.PrefetchScalarGridSpec(
            num_scalar_prefetch=1,
            grid=(batch, N_HEADS),
            in_specs=[spec(SLAB_QR), spec(SLAB_KR), spec(SLAB_VR), spec(SLAB_ZR)],
            out_specs=pl.BlockSpec((seq, LANE), lambda b, h, lg: (b, h)),
        ),
        out_shape=jax.ShapeDtypeStruct((m, WIDTH), BF16),
        compiler_params=pltpu.CompilerParams(
            dimension_semantics=("parallel", "parallel"), vmem_limit_bytes=VMEM_LIMIT),
        name="retention",
    )(log_gamma, proj, proj, proj, proj)


def _merge_kernel(oa_ref, oc_ref, or_ref, ga_ref, gc_ref, gr_ref, wa_ref, wc_ref, wr_ref, o_ref):
    ya = _dot(oa_ref[...], wa_ref[...].astype(BF16))
    yc = _dot(oc_ref[...], wc_ref[...].astype(BF16))
    yr = _dot(or_ref[...], wr_ref[...].astype(BF16))
    for s in range(ga_ref.shape[0]):
        lanes = slice(s * LANE, (s + 1) * LANE)
        merged = (ga_ref[s].astype(F32) * ya[:, lanes] + gc_ref[s].astype(F32) * yc[:, lanes]
                  + gr_ref[s].astype(F32) * yr[:, lanes])
        o_ref[:, lanes] = merged.astype(BF16)


def _merge(proj, oa, oc, orr, wa, wc, wr, layer, tm=1024, tn=512):
    m = proj.shape[1]
    gs = tn // LANE

    def gate(base):
        return pl.BlockSpec((gs, tm, LANE), lambda n, i: (base // gs + n, i, 0))

    wspec = pl.BlockSpec((None, WIDTH, tn), lambda n, i: (layer, 0, n))
    own = pl.BlockSpec((tm, WIDTH), lambda n, i: (i, 0))
    return pl.pallas_call(
        _merge_kernel,
        grid=(D_MODEL // tn, m // tm),
        in_specs=[own, own, own, gate(SLAB_GA), gate(SLAB_GC), gate(SLAB_GR), wspec, wspec, wspec],
        out_specs=pl.BlockSpec((tm, tn), lambda n, i: (i, n)),
        out_shape=jax.ShapeDtypeStruct((m, D_MODEL), BF16),
        compiler_params=pltpu.CompilerParams(
            dimension_semantics=("parallel", "parallel"), vmem_limit_bytes=VMEM_LIMIT),
        name="branch_merge",
    )(oa, oc, orr, proj, proj, proj, wa, wc, wr)


def _out_proj_kernel(m_ref, w_ref, x_ref, g_ref, *rest):
    y = _dot(m_ref[...], w_ref[...].astype(BF16))
    ms = jnp.mean(y * y, axis=-1, keepdims=True)
    x_new = x_ref[...] + y * lax.rsqrt(ms + NORM_EPS) * g_ref[...]
    if len(rest) == 1:
        (o_ref,) = rest
    else:
        next_g_ref, o_ref, h_ref = rest
        ms = jnp.mean(x_new * x_new, axis=-1, keepdims=True)
        h_ref[...] = (x_new * lax.rsqrt(ms + NORM_EPS) * next_g_ref[...]).astype(BF16)
    o_ref[...] = x_new


def _out_proj(merged, w_all, layer, x2, g, next_g=None, tm=512):
    m, d = x2.shape
    rows = pl.BlockSpec((tm, d), lambda i: (i, 0))
    gain = pl.BlockSpec((1, d), lambda i: (0, 0))
    more = next_g is not None
    out = pl.pallas_call(
        _out_proj_kernel,
        grid=(m // tm,),
        in_specs=[rows,
                  pl.BlockSpec((None, d, d), lambda i: (layer, 0, 0), pipeline_mode=pl.Buffered(1)),
                  rows, gain] + [gain] * more,
        out_specs=[rows] + [rows] * more,
        out_shape=[jax.ShapeDtypeStruct((m, d), F32)] + [jax.ShapeDtypeStruct((m, d), BF16)] * more,
        compiler_params=pltpu.CompilerParams(
            dimension_semantics=("parallel",), vmem_limit_bytes=VMEM_LIMIT),
        name="out_proj",
    )(merged, w_all, x2, g.reshape(1, d), *([next_g.reshape(1, d)] if more else []))
    return (out[0], out[1]) if more else (out[0], None)


def _rotary_tables(seq):
    half = HEAD_DIM // 2
    inv_freq = ROPE_BASE ** (-jnp.arange(half, dtype=F32) / half)
    ang = jnp.arange(seq, dtype=F32)[:, None] * inv_freq[None, :]
    cos, sin = jnp.cos(ang), jnp.sin(ang)
    return jnp.concatenate([cos, cos], axis=-1), jnp.concatenate([-sin, sin], axis=-1)


def kernel(x, pre_norm_g, post_norm_g, w_in, conv_w, conv_b, w_branch_a, w_branch_c, w_branch_r, w_out):
    batch, seq, d = x.shape
    assert d == D_MODEL and w_in.shape[-1] == N_IN
    assert seq % (max(DILATIONS) * ATTN_TILE) == 0 and seq % RET_CHUNK == 0
    cos2, sin2 = _rotary_tables(seq)
    log_gamma = jnp.log1p(-jnp.exp2(-5.0 - jnp.arange(N_HEADS, dtype=F32)))
    x2 = x.reshape(batch * seq, d)
    depth = w_in.shape[0]
    h = _pre_norm(x2, pre_norm_g[0])
    for layer in range(depth):
        attn_nat, attn_cm = _attn_proj(h, w_in, layer, batch, seq)
        oa = _attention(attn_nat, attn_cm, batch, seq)
        proj = _in_proj(h, w_in, layer, cos2, sin2, seq)
        orr = _retention(proj, log_gamma, batch, seq)
        oc = _conv_proj(h, w_in, conv_w[layer], conv_b[layer], layer, seq)
        merged = _merge(proj, oa, oc, orr, w_branch_a, w_branch_c, w_branch_r, layer)
        next_g = pre_norm_g[layer + 1] if layer + 1 < depth else None
        x2, h = _out_proj(merged, w_out, layer, x2, post_norm_g[layer], next_g)
    return x2.reshape(batch, seq, d)
```

```python
import functools
import math

import jax
import jax.numpy as jnp
from jax import lax
from jax.experimental import pallas as pl
from jax.experimental.pallas import tpu as pltpu

F32 = jnp.float32
BF16 = jnp.bfloat16

LANE = 128
D_MODEL = 2048
HEAD_DIM = 128
N_HEADS = 12
WIDTH = N_HEADS * HEAD_DIM
N_IN = 12 * WIDTH + 3 * D_MODEL
DILATIONS = (1, 4, 16)
ATTN_TILE = 128
ATTN_BATCH = {1: 16, 4: 32, 16: 32}
LOG2_E = math.log2(math.e)
RET_CHUNK = 128
RET_BATCH = 32
CONV_K = 3
ROPE_BASE = 10000.0
NORM_EPS = 1e-6
NEG_BIG = -1e30

COL_QA, COL_KA, COL_VA, COL_ZA = 0, 12, 24, 36
COL_UC, COL_BC, COL_CC, COL_ZC = 48, 60, 72, 84
COL_QR, COL_KR, COL_VR, COL_ZR = 96, 108, 120, 132
COL_GA, COL_GC, COL_GR = 144, 160, 176
ATTN_SLABS = COL_UC - COL_QA
SLAB_QA, SLAB_KA, SLAB_VA, SLAB_ZA = COL_QA, COL_KA, COL_VA, COL_ZA
SLAB_QR, SLAB_KR, SLAB_VR, SLAB_ZR = (c - COL_QR for c in (COL_QR, COL_KR, COL_VR, COL_ZR))
SLAB_GA, SLAB_GC, SLAB_GR = (c - COL_QR for c in (COL_GA, COL_GC, COL_GR))
N_CLASSES = 4

VMEM_LIMIT = 56 * 1024 * 1024


def _sigmoid(x):
    return 0.5 * jnp.tanh(0.5 * x) + 0.5


def _silu(x):
    return x * _sigmoid(x)


def _dot(a, b):
    return jnp.dot(a, b, preferred_element_type=F32)


def _pre_norm_kernel(x_ref, g_ref, o_ref):
    xf = x_ref[...]
    ms = jnp.mean(xf * xf, axis=-1, keepdims=True)
    o_ref[...] = (xf * lax.rsqrt(ms + NORM_EPS) * g_ref[...]).astype(BF16)


def _pre_norm(x2, g, tm=512):
    m, d = x2.shape
    return pl.pallas_call(
        _pre_norm_kernel,
        grid=(m // tm,),
        in_specs=[pl.BlockSpec((tm, d), lambda i: (i, 0)), pl.BlockSpec((1, d), lambda i: (0, 0))],
        out_specs=pl.BlockSpec((tm, d), lambda i: (i, 0)),
        out_shape=jax.ShapeDtypeStruct((m, d), BF16),
        compiler_params=pltpu.CompilerParams(
            dimension_semantics=("parallel",), vmem_limit_bytes=VMEM_LIMIT),
        name="pre_norm",
    )(x2, g.reshape(1, d))


def _in_proj_kernel(h_ref, w_ref, o_ref):
    acc = _dot(h_ref[...], w_ref[...].astype(BF16))
    for c in range(o_ref.shape[0]):
        o_ref[c] = acc[:, c * LANE:(c + 1) * LANE].astype(BF16)


def _in_proj(h, w_all, layer, col0, n_slabs, tm=2048, tn=1024):
    m, d = h.shape
    tile0 = col0 * LANE // tn
    return pl.pallas_call(
        _in_proj_kernel,
        grid=(n_slabs * LANE // tn, m // tm),
        in_specs=[
            pl.BlockSpec((tm, d), lambda j, i: (i, 0)),
            pl.BlockSpec((None, d, tn), lambda j, i: (layer, 0, tile0 + j)),
        ],
        out_specs=pl.BlockSpec((tn // LANE, tm, LANE), lambda j, i: (j, i, 0)),
        out_shape=jax.ShapeDtypeStruct((n_slabs, m, LANE), BF16),
        compiler_params=pltpu.CompilerParams(
            dimension_semantics=("parallel", "parallel"), vmem_limit_bytes=VMEM_LIMIT),
        name="in_proj",
    )(h, w_all)


def _attn_proj_kernel(h_ref, w_ref, nat_ref, cm_ref, scr_ref):
    j = pl.program_id(0)
    n_slabs, n_cls, rows = cm_ref.shape[0], cm_ref.shape[1], cm_ref.shape[2]
    pair = 2 * LANE
    for c0 in range(0, n_slabs, 2):
        acc = _dot(h_ref[...], w_ref[:, c0 * LANE:c0 * LANE + pair].astype(BF16))
        for c in (c0, c0 + 1):
            is_q = (j * n_slabs + c) < N_HEADS
            slab = (acc[:, (c - c0) * LANE:(c - c0 + 1) * LANE]
                    * jnp.where(is_q, HEAD_DIM ** -0.5 * LOG2_E, 1.0))
            nat_ref[c] = slab.astype(BF16)
            scr_ref[c] = slab
            for r in range(n_cls):
                cm_ref[c, r] = scr_ref[c, pl.ds(r, rows, stride=n_cls), :]


def _attn_proj(h, w_all, layer, batch, seq, tm=1024, tn=1024):
    m, d = h.shape
    tiles_per_seq = seq // tm
    slabs = tn // LANE
    return pl.pallas_call(
        _attn_proj_kernel,
        grid=(ATTN_SLABS * LANE // tn, m // tm),
        in_specs=[
            pl.BlockSpec((tm, d), lambda j, i: (i, 0)),
            pl.BlockSpec((None, d, tn), lambda j, i: (layer, 0, COL_QA * LANE // tn + j)),
        ],
        out_specs=[
            pl.BlockSpec((slabs, tm, LANE), lambda j, i: (j, i, 0)),
            pl.BlockSpec((slabs, None, N_CLASSES, tm // N_CLASSES, LANE),
                         lambda j, i: (j, i // tiles_per_seq, 0, i % tiles_per_seq, 0)),
        ],
        out_shape=[jax.ShapeDtypeStruct((ATTN_SLABS, m, LANE), BF16),
                   jax.ShapeDtypeStruct((ATTN_SLABS, batch, N_CLASSES, seq // N_CLASSES, LANE), F32)],
        scratch_shapes=[pltpu.VMEM((slabs, tm, LANE), F32)],
        compiler_params=pltpu.CompilerParams(
            dimension_semantics=("parallel", "parallel"), vmem_limit_bytes=VMEM_LIMIT),
        name="attn_proj",
    )(h, w_all)


def _conv_proj_kernel(h_ref, wu_ref, wb_ref, wc_ref, wz_ref, cw_ref, cb_ref, o_ref, halo_ref,
                      *, tiles_per_seq):
    i = pl.program_id(1)
    tm = h_ref.shape[0]

    @pl.when(i == 0)
    def _():
        halo_ref[...] = jnp.zeros(halo_ref.shape, F32)

    h = h_ref[...]
    cu = _dot(h, wc_ref[...].astype(BF16)) * _dot(h, wu_ref[...].astype(BF16))
    halo = jnp.where((i % tiles_per_seq) == 0, 0.0, halo_ref[...])
    halo_ref[...] = cu[tm - 8:, :]
    row = lax.broadcasted_iota(jnp.int32, (8, cu.shape[1]), 0)
    back1 = pltpu.roll(cu, 1, 0)
    back2 = pltpu.roll(cu, 2, 0)
    head1 = jnp.where(row == 0, halo[7:8, :], back1[:8])
    head2 = jnp.where(row == 0, halo[6:7, :], jnp.where(row == 1, halo[7:8, :], back2[:8]))
    back1 = jnp.concatenate([head1, back1[8:]], axis=0)
    back2 = jnp.concatenate([head2, back2[8:]], axis=0)
    conv = (cb_ref[...] + cw_ref[2:3, :] * cu + cw_ref[0:1, :] * back2 + cw_ref[1:2, :] * back1)
    hz = 0.5 * _dot(h, wz_ref[...].astype(BF16))
    gated = (conv * (_dot(h, wb_ref[...].astype(BF16)) * hz)) * (1.0 + jnp.tanh(hz))
    o_ref[...] = gated.astype(BF16)


def _conv_proj(h, w_all, conv_w, conv_b, layer, seq, tm=1024, tc=256):
    m, d = h.shape

    def wspec(col):
        return pl.BlockSpec((None, d, tc), lambda jc, i: (layer, 0, col * LANE // tc + jc))

    return pl.pallas_call(
        functools.partial(_conv_proj_kernel, tiles_per_seq=seq // tm),
        grid=(WIDTH // tc, m // tm),
        in_specs=[pl.BlockSpec((tm, d), lambda jc, i: (i, 0)),
                  wspec(COL_UC), wspec(COL_BC), wspec(COL_CC), wspec(COL_ZC),
                  pl.BlockSpec((CONV_K, tc), lambda jc, i: (0, jc)),
                  pl.BlockSpec((1, tc), lambda jc, i: (0, jc))],
        out_specs=pl.BlockSpec((tm, tc), lambda jc, i: (i, jc)),
        out_shape=jax.ShapeDtypeStruct((m, WIDTH), BF16),
        scratch_shapes=[pltpu.VMEM((8, tc), F32)],
        compiler_params=pltpu.CompilerParams(
            dimension_semantics=("parallel", "arbitrary"), vmem_limit_bytes=VMEM_LIMIT),
        name="conv_proj",
    )(h, w_all, w_all, w_all, w_all, conv_w, conv_b.reshape(1, WIDTH))


def _attn_kernel(qn_ref, kn_ref, vn_ref, zn_ref, qc_ref, kc_ref, vc_ref, o_ref,
                 o4, lse4, o16, lse16):
    seq = qn_ref.shape[1]
    t = ATTN_TILE
    e = HEAD_DIM
    n_cls = qc_ref.shape[0]
    row = lax.broadcasted_iota(jnp.int32, (t, t), 0)
    col = lax.broadcasted_iota(jnp.int32, (t, t), 1)

    def bdot(a, b, contract_b):
        return lax.dot_general(a, b, (((2,), (contract_b,)), ((0,), (0,))),
                               preferred_element_type=F32)

    def partials(q, k_cur, k_prev, v_cur, v_prev, cur_ok, prev_ok):
        g = q.shape[0]
        keys = jnp.concatenate([k_prev, k_cur], axis=1)
        vals = jnp.concatenate([v_prev, v_cur], axis=1)
        ok = jnp.concatenate([jnp.broadcast_to(prev_ok, (g, t, t)),
                              jnp.broadcast_to(cur_ok, (g, t, t))], axis=2)
        s = jnp.where(ok, bdot(q, keys, 2), NEG_BIG)
        m = jnp.max(s, axis=-1, keepdims=True)
        p = jnp.exp2(s - m).astype(BF16)
        acc_l = bdot(p, jnp.concatenate([vals, jnp.ones(vals.shape, BF16)], axis=2), 1)
        return acc_l[:, :, :e], m, acc_l[:, :, e:]

    def previous_tiles(x):
        return jnp.concatenate([x[:, :1], x[:, :-1]], axis=1)

    def dilated_pass(dil, o_s, lse_s):
        step = dil // n_cls
        tiles = seq // (dil * t)
        classes = min(ATTN_BATCH[dil] // tiles, dil)
        g = classes * tiles
        slot = lax.broadcasted_iota(jnp.int32, (g, t, t), 0)
        cur_ok = col <= row
        prev_ok = jnp.logical_and(col >= row, slot % tiles > 0)

        def class_rows(ref, r):
            if step == 1:
                return ref[r]
            return ref[r % n_cls, pl.ds(r // n_cls, tiles * t, stride=step), :]

        def gather(ref, r0):
            return jnp.stack([class_rows(ref, r0 + c).astype(BF16).reshape(tiles, t, e)
                              for c in range(classes)], axis=0)

        def body(i, carry):
            r0 = i * classes
            k, v = gather(kc_ref, r0), gather(vc_ref, r0)
            acc, m, l = partials(gather(qc_ref, r0).reshape(g, t, e),
                                 k.reshape(g, t, e), previous_tiles(k).reshape(g, t, e),
                                 v.reshape(g, t, e), previous_tiles(v).reshape(g, t, e),
                                 cur_ok, prev_ok)
            out = acc / l
            lse = m + jnp.log2(l)
            for c in range(classes):
                dst = pl.ds(r0 + c, tiles * t, stride=dil)
                part = lambda x: x[c * tiles:(c + 1) * tiles].reshape(tiles * t, e)
                o_s[dst, :] = part(out)
                lse_s[dst, :] = part(lse)
            return carry

        lax.fori_loop(0, dil // classes, body, 0)

    dilated_pass(4, o4, lse4)
    dilated_pass(16, o16, lse16)

    g = ATTN_BATCH[1]
    slot = lax.broadcasted_iota(jnp.int32, (g, t, t), 0)
    for i in range(seq // (g * t)):
        start = i * g * t
        rows = slice(start, start + g * t)
        if i == 0:
            with_prev = lambda ref: jnp.concatenate([ref[0, :t], ref[0, rows]], axis=0)
            prev_ok = jnp.logical_and(col >= row, slot > 0)
        else:
            with_prev = lambda ref: ref[0, start - t:start + g * t]
            prev_ok = col >= row
        k_all, v_all = with_prev(kn_ref), with_prev(vn_ref)
        acc1, m1, l1 = partials(qn_ref[0, rows].reshape(g, t, e),
                                k_all[t:].reshape(g, t, e), k_all[:g * t].reshape(g, t, e),
                                v_all[t:].reshape(g, t, e), v_all[:g * t].reshape(g, t, e),
                                col <= row, prev_ok)
        load = lambda ref: ref[rows, :].reshape(g, t, e)
        mm4, mm16 = load(lse4), load(lse16)
        m = jnp.maximum(jnp.maximum(m1, mm4), mm16)
        w1 = jnp.exp2(m1 - m)
        w4 = jnp.exp2(mm4 - m)
        w16 = jnp.exp2(mm16 - m)
        num = w1 * acc1 + w4 * load(o4) + w16 * load(o16)
        den = w1 * l1 + w4 + w16
        z = zn_ref[0, rows].astype(F32).reshape(g, t, e)
        o_ref[rows, :] = (num / den * _silu(z)).astype(BF16).reshape(g * t, e)


def _attention(nat, cm, batch, seq):
    m = nat.shape[1]

    def spec(base):
        return pl.BlockSpec((1, seq, LANE), lambda b, h: (base + h, b, 0))

    def cm_spec(base):
        return pl.BlockSpec((None, None, N_CLASSES, seq // N_CLASSES, LANE),
                            lambda b, h: (base + h, b, 0, 0, 0))

    return pl.pallas_call(
        _attn_kernel,
        grid=(batch, N_HEADS),
        in_specs=[spec(SLAB_QA), spec(SLAB_KA), spec(SLAB_VA), spec(SLAB_ZA),
                  cm_spec(SLAB_QA), cm_spec(SLAB_KA), cm_spec(SLAB_VA)],
        out_specs=pl.BlockSpec((seq, LANE), lambda b, h: (b, h)),
        out_shape=jax.ShapeDtypeStruct((m, WIDTH), BF16),
        scratch_shapes=[pltpu.VMEM((seq, LANE), F32)] * 4,
        compiler_params=pltpu.CompilerParams(
            dimension_semantics=("parallel", "parallel"), vmem_limit_bytes=VMEM_LIMIT),
        name="dilated_attn",
    )(nat, nat, nat, nat, cm, cm, cm)


def _ret_kernel(lg_ref, q_ref, k_ref, v_ref, z_ref, cos_ref, sin_ref, o_ref):
    seq = q_ref.shape[1]
    c = RET_CHUNK
    lg = lg_ref[pl.program_id(1)]
    row = lax.broadcasted_iota(jnp.int32, (c, c), 0).astype(F32)
    col = lax.broadcasted_iota(jnp.int32, (c, c), 1).astype(F32)
    rel = row - col
    key_scale = HEAD_DIM ** -0.5
    decay_mask = jnp.where(rel >= 0, jnp.exp(jnp.maximum(rel, 0.0) * lg), 0.0) * key_scale
    q_decay = jnp.exp((row + 1.0) * lg)
    k_decay = jnp.exp((c - 1.0 - row) * lg) * key_scale
    chunk_decay = jnp.exp(jnp.full((c, c), float(c), F32) * lg)

    g = min(RET_BATCH, seq // c)
    e = HEAD_DIM

    def bdot(a, b, contract_a, contract_b):
        return lax.dot_general(a, b, (((contract_a,), (contract_b,)), ((0,), (0,))),
                               preferred_element_type=F32)

    def body(i, state):
        sl = pl.ds(pl.multiple_of(i * (g * c), g * c), g * c)
        cos, sin = cos_ref[sl, :], sin_ref[sl, :]
        rotate = lambda x: x * cos + pltpu.roll(x, e // 2, 1) * sin
        q = rotate(q_ref[0, sl, :].astype(F32)).reshape(g, c, e)
        k = rotate(k_ref[0, sl, :].astype(F32)).reshape(g, c, e)
        v = v_ref[0, sl, :].reshape(g, c, e)
        inner = bdot(q.astype(BF16), k.astype(BF16), 2, 2) * decay_mask
        kv = bdot((k * k_decay).astype(BF16), v, 1, 1)
        states = []
        for j in range(g):
            states.append(state)
            state = state * chunk_decay + kv[j]
        lhs = jnp.concatenate([inner.astype(BF16), (q * q_decay).astype(BF16)], axis=2)
        rhs = jnp.concatenate([v, jnp.stack(states).astype(BF16)], axis=1)
        o = bdot(lhs, rhs, 2, 1)
        o = o * lax.rsqrt(jnp.mean(o * o, axis=-1, keepdims=True) + NORM_EPS)
        z = z_ref[0, sl, :].astype(F32).reshape(g, c, e)
        o_ref[sl, :] = (o * _silu(z)).astype(BF16).reshape(g * c, e)
        return state

    lax.fori_loop(0, seq // (g * c), body, jnp.zeros((e, e), F32))


def _retention(proj, log_gamma, cos2, sin2, batch, seq):
    m = proj.shape[1]

    def spec(base):
        return pl.BlockSpec((1, seq, LANE), lambda b, h, lg: (base + h, b, 0))

    table = pl.BlockSpec((seq, LANE), lambda b, h, lg: (0, 0))
    return pl.pallas_call(
        _ret_kernel,
        grid_spec=pltpu.PrefetchScalarGridSpec(
            num_scalar_prefetch=1,
            grid=(batch, N_HEADS),
            in_specs=[spec(SLAB_QR), spec(SLAB_KR), spec(SLAB_VR), spec(SLAB_ZR), table, table],
            out_specs=pl.BlockSpec((seq, LANE), lambda b, h, lg: (b, h)),
        ),
        out_shape=jax.ShapeDtypeStruct((m, WIDTH), BF16),
        compiler_params=pltpu.CompilerParams(
            dimension_semantics=("parallel", "parallel"), vmem_limit_bytes=VMEM_LIMIT),
        name="retention",
    )(log_gamma, proj, proj, proj, proj, cos2, sin2)


def _merge_kernel(oa_ref, oc_ref, or_ref, ga_ref, gc_ref, gr_ref, wa_ref, wc_ref, wr_ref, o_ref):
    ya = _dot(oa_ref[...], wa_ref[...].astype(BF16))
    yc = _dot(oc_ref[...], wc_ref[...].astype(BF16))
    yr = _dot(or_ref[...], wr_ref[...].astype(BF16))
    for s in range(ga_ref.shape[0]):
        lanes = slice(s * LANE, (s + 1) * LANE)
        merged = (_sigmoid(ga_ref[s].astype(F32)) * ya[:, lanes]
                  + _sigmoid(gc_ref[s].astype(F32)) * yc[:, lanes]
                  + _sigmoid(gr_ref[s].astype(F32)) * yr[:, lanes])
        o_ref[:, lanes] = merged.astype(BF16)


def _merge(proj, oa, oc, orr, wa, wc, wr, layer, tm=1024, tn=512):
    m = proj.shape[1]
    gs = tn // LANE

    def gate(base):
        return pl.BlockSpec((gs, tm, LANE), lambda n, i: (base // gs + n, i, 0))

    wspec = pl.BlockSpec((None, WIDTH, tn), lambda n, i: (layer, 0, n))
    own = pl.BlockSpec((tm, WIDTH), lambda n, i: (i, 0))
    return pl.pallas_call(
        _merge_kernel,
        grid=(D_MODEL // tn, m // tm),
        in_specs=[own, own, own, gate(SLAB_GA), gate(SLAB_GC), gate(SLAB_GR), wspec, wspec, wspec],
        out_specs=pl.BlockSpec((tm, tn), lambda n, i: (i, n)),
        out_shape=jax.ShapeDtypeStruct((m, D_MODEL), BF16),
        compiler_params=pltpu.CompilerParams(
            dimension_semantics=("parallel", "parallel"), vmem_limit_bytes=VMEM_LIMIT),
        name="branch_merge",
    )(oa, oc, orr, proj, proj, proj, wa, wc, wr)


def _out_proj_kernel(m_ref, w_ref, x_ref, g_ref, *rest):
    y = _dot(m_ref[...], w_ref[...].astype(BF16))
    ms = jnp.mean(y * y, axis=-1, keepdims=True)
    x_new = x_ref[...] + y * lax.rsqrt(ms + NORM_EPS) * g_ref[...]
    if len(rest) == 1:
        (o_ref,) = rest
    else:
        next_g_ref, o_ref, h_ref = rest
        ms = jnp.mean(x_new * x_new, axis=-1, keepdims=True)
        h_ref[...] = (x_new * lax.rsqrt(ms + NORM_EPS) * next_g_ref[...]).astype(BF16)
    o_ref[...] = x_new


def _out_proj(merged, w_all, layer, x2, g, next_g=None, tm=512):
    m, d = x2.shape
    rows = pl.BlockSpec((tm, d), lambda i: (i, 0))
    gain = pl.BlockSpec((1, d), lambda i: (0, 0))
    more = next_g is not None
    out = pl.pallas_call(
        _out_proj_kernel,
        grid=(m // tm,),
        in_specs=[rows,
                  pl.BlockSpec((None, d, d), lambda i: (layer, 0, 0), pipeline_mode=pl.Buffered(1)),
                  rows, gain] + [gain] * more,
        out_specs=[rows] + [rows] * more,
        out_shape=[jax.ShapeDtypeStruct((m, d), F32)] + [jax.ShapeDtypeStruct((m, d), BF16)] * more,
        compiler_params=pltpu.CompilerParams(
            dimension_semantics=("parallel",), vmem_limit_bytes=VMEM_LIMIT),
        name="out_proj",
    )(merged, w_all, x2, g.reshape(1, d), *([next_g.reshape(1, d)] if more else []))
    return (out[0], out[1]) if more else (out[0], None)


def _rotary_tables(seq):
    half = HEAD_DIM // 2
    inv_freq = ROPE_BASE ** (-jnp.arange(half, dtype=F32) / half)
    ang = jnp.arange(seq, dtype=F32)[:, None] * inv_freq[None, :]
    cos, sin = jnp.cos(ang), jnp.sin(ang)
    return jnp.concatenate([cos, cos], axis=-1), jnp.concatenate([-sin, sin], axis=-1)


def kernel(x, pre_norm_g, post_norm_g, w_in, conv_w, conv_b, w_branch_a, w_branch_c, w_branch_r, w_out):
    batch, seq, d = x.shape
    assert d == D_MODEL and w_in.shape[-1] == N_IN
    assert seq % (max(DILATIONS) * ATTN_TILE) == 0 and seq % RET_CHUNK == 0
    cos2, sin2 = _rotary_tables(seq)
    log_gamma = jnp.log1p(-jnp.exp2(-5.0 - jnp.arange(N_HEADS, dtype=F32)))
    x2 = x.reshape(batch * seq, d)
    depth = w_in.shape[0]
    h = _pre_norm(x2, pre_norm_g[0])
    for layer in range(depth):
        attn_nat, attn_cm = _attn_proj(h, w_in, layer, batch, seq)
        oa = _attention(attn_nat, attn_cm, batch, seq)
        proj = _in_proj(h, w_in, layer, COL_QR, N_IN // LANE - COL_QR)
        orr = _retention(proj, log_gamma, cos2, sin2, batch, seq)
        oc = _conv_proj(h, w_in, conv_w[layer], conv_b[layer], layer, seq)
        merged = _merge(proj, oa, oc, orr, w_branch_a, w_branch_c, w_branch_r, layer)
        next_g = pre_norm_g[layer + 1] if layer + 1 < depth else None
        x2, h = _out_proj(merged, w_out, layer, x2, post_norm_g[layer], next_g)
    return x2.reshape(batch, seq, d)
```

```python
import functools
import math

import jax
import jax.numpy as jnp
from jax import lax
from jax.experimental import pallas as pl
from jax.experimental.pallas import tpu as pltpu

F32 = jnp.float32
BF16 = jnp.bfloat16

LANE = 128
D_MODEL = 2048
HEAD_DIM = 128
N_HEADS = 12
WIDTH = N_HEADS * HEAD_DIM
N_IN = 12 * WIDTH + 3 * D_MODEL
DILATIONS = (1, 4, 16)
ATTN_TILE = 128
ATTN_BATCH = {1: 32, 4: 32, 16: 32}
LOG2_E = math.log2(math.e)
RET_CHUNK = 128
RET_BATCH = 32
CONV_K = 3
ROPE_BASE = 10000.0
NORM_EPS = 1e-6
NEG_BIG = -1e30

COL_QA, COL_KA, COL_VA, COL_ZA = 0, 12, 24, 36
COL_UC, COL_BC, COL_CC, COL_ZC = 48, 60, 72, 84
COL_QR, COL_KR, COL_VR, COL_ZR = 96, 108, 120, 132
COL_GA, COL_GC, COL_GR = 144, 160, 176
ATTN_SLABS = COL_UC - COL_QA
SLAB_QA, SLAB_KA, SLAB_VA, SLAB_ZA = COL_QA, COL_KA, COL_VA, COL_ZA
SLAB_QR, SLAB_KR, SLAB_VR, SLAB_ZR = (c - COL_QR for c in (COL_QR, COL_KR, COL_VR, COL_ZR))
SLAB_GA, SLAB_GC, SLAB_GR = (c - COL_QR for c in (COL_GA, COL_GC, COL_GR))
N_CLASSES = 4

VMEM_LIMIT = 56 * 1024 * 1024


def _sigmoid(x):
    return 0.5 * jnp.tanh(0.5 * x) + 0.5


def _silu(x):
    return x * _sigmoid(x)


def _dot(a, b):
    return jnp.dot(a, b, preferred_element_type=F32)


def _pre_norm_kernel(x_ref, g_ref, o_ref):
    xf = x_ref[...]
    ms = jnp.mean(xf * xf, axis=-1, keepdims=True)
    o_ref[...] = (xf * lax.rsqrt(ms + NORM_EPS) * g_ref[...]).astype(BF16)


def _pre_norm(x2, g, tm=512):
    m, d = x2.shape
    return pl.pallas_call(
        _pre_norm_kernel,
        grid=(m // tm,),
        in_specs=[pl.BlockSpec((tm, d), lambda i: (i, 0)), pl.BlockSpec((1, d), lambda i: (0, 0))],
        out_specs=pl.BlockSpec((tm, d), lambda i: (i, 0)),
        out_shape=jax.ShapeDtypeStruct((m, d), BF16),
        compiler_params=pltpu.CompilerParams(
            dimension_semantics=("parallel",), vmem_limit_bytes=VMEM_LIMIT),
        name="pre_norm",
    )(x2, g.reshape(1, d))


def _in_proj_kernel(h_ref, w_ref, o_ref):
    acc = _dot(h_ref[...], w_ref[...].astype(BF16))
    for c in range(o_ref.shape[0]):
        o_ref[c] = acc[:, c * LANE:(c + 1) * LANE].astype(BF16)


def _in_proj(h, w_all, layer, col0, n_slabs, tm=2048, tn=1024):
    m, d = h.shape
    tile0 = col0 * LANE // tn
    return pl.pallas_call(
        _in_proj_kernel,
        grid=(n_slabs * LANE // tn, m // tm),
        in_specs=[
            pl.BlockSpec((tm, d), lambda j, i: (i, 0)),
            pl.BlockSpec((None, d, tn), lambda j, i: (layer, 0, tile0 + j)),
        ],
        out_specs=pl.BlockSpec((tn // LANE, tm, LANE), lambda j, i: (j, i, 0)),
        out_shape=jax.ShapeDtypeStruct((n_slabs, m, LANE), BF16),
        compiler_params=pltpu.CompilerParams(
            dimension_semantics=("parallel", "parallel"), vmem_limit_bytes=VMEM_LIMIT),
        name="in_proj",
    )(h, w_all)


def _attn_proj_kernel(h_ref, w_ref, nat_ref, cm_ref, scr_ref):
    j = pl.program_id(0)
    n_slabs, n_cls, rows = cm_ref.shape[0], cm_ref.shape[1], cm_ref.shape[2]
    pair = 2 * LANE
    for c0 in range(0, n_slabs, 2):
        acc = _dot(h_ref[...], w_ref[:, c0 * LANE:c0 * LANE + pair].astype(BF16))
        for c in (c0, c0 + 1):
            is_q = (j * n_slabs + c) < N_HEADS
            slab = (acc[:, (c - c0) * LANE:(c - c0 + 1) * LANE]
                    * jnp.where(is_q, HEAD_DIM ** -0.5 * LOG2_E, 1.0))
            nat_ref[c] = slab.astype(BF16)
            scr_ref[c] = slab
            for r in range(n_cls):
                cm_ref[c, r] = scr_ref[c, pl.ds(r, rows, stride=n_cls), :]


def _attn_proj(h, w_all, layer, batch, seq, tm=1024, tn=1024):
    m, d = h.shape
    tiles_per_seq = seq // tm
    slabs = tn // LANE
    return pl.pallas_call(
        _attn_proj_kernel,
        grid=(ATTN_SLABS * LANE // tn, m // tm),
        in_specs=[
            pl.BlockSpec((tm, d), lambda j, i: (i, 0)),
            pl.BlockSpec((None, d, tn), lambda j, i: (layer, 0, COL_QA * LANE // tn + j)),
        ],
        out_specs=[
            pl.BlockSpec((slabs, tm, LANE), lambda j, i: (j, i, 0)),
            pl.BlockSpec((slabs, None, N_CLASSES, tm // N_CLASSES, LANE),
                         lambda j, i: (j, i // tiles_per_seq, 0, i % tiles_per_seq, 0)),
        ],
        out_shape=[jax.ShapeDtypeStruct((ATTN_SLABS, m, LANE), BF16),
                   jax.ShapeDtypeStruct((ATTN_SLABS, batch, N_CLASSES, seq // N_CLASSES, LANE), F32)],
        scratch_shapes=[pltpu.VMEM((slabs, tm, LANE), F32)],
        compiler_params=pltpu.CompilerParams(
            dimension_semantics=("parallel", "parallel"), vmem_limit_bytes=VMEM_LIMIT),
        name="attn_proj",
    )(h, w_all)


def _conv_proj_kernel(h_ref, wu_ref, wb_ref, wc_ref, wz_ref, cw_ref, cb_ref, o_ref, halo_ref,
                      *, tiles_per_seq):
    i = pl.program_id(1)
    tm = h_ref.shape[0]

    @pl.when(i == 0)
    def _():
        halo_ref[...] = jnp.zeros(halo_ref.shape, F32)

    h = h_ref[...]
    cu = _dot(h, wc_ref[...].astype(BF16)) * _dot(h, wu_ref[...].astype(BF16))
    halo = jnp.where((i % tiles_per_seq) == 0, 0.0, halo_ref[...])
    halo_ref[...] = cu[tm - 8:, :]
    row = lax.broadcasted_iota(jnp.int32, (8, cu.shape[1]), 0)
    back1 = pltpu.roll(cu, 1, 0)
    back2 = pltpu.roll(cu, 2, 0)
    head1 = jnp.where(row == 0, halo[7:8, :], back1[:8])
    head2 = jnp.where(row == 0, halo[6:7, :], jnp.where(row == 1, halo[7:8, :], back2[:8]))
    back1 = jnp.concatenate([head1, back1[8:]], axis=0)
    back2 = jnp.concatenate([head2, back2[8:]], axis=0)
    conv = (cb_ref[...] + cw_ref[2:3, :] * cu + cw_ref[0:1, :] * back2 + cw_ref[1:2, :] * back1)
    hz = 0.5 * _dot(h, wz_ref[...].astype(BF16))
    gated = (conv * (_dot(h, wb_ref[...].astype(BF16)) * hz)) * (1.0 + jnp.tanh(hz))
    o_ref[...] = gated.astype(BF16)


def _conv_proj(h, w_all, conv_w, conv_b, layer, seq, tm=2048, tc=256):
    m, d = h.shape

    def wspec(col):
        return pl.BlockSpec((None, d, tc), lambda jc, i: (layer, 0, col * LANE // tc + jc))

    return pl.pallas_call(
        functools.partial(_conv_proj_kernel, tiles_per_seq=seq // tm),
        grid=(WIDTH // tc, m // tm),
        in_specs=[pl.BlockSpec((tm, d), lambda jc, i: (i, 0)),
                  wspec(COL_UC), wspec(COL_BC), wspec(COL_CC), wspec(COL_ZC),
                  pl.BlockSpec((CONV_K, tc), lambda jc, i: (0, jc)),
                  pl.BlockSpec((1, tc), lambda jc, i: (0, jc))],
        out_specs=pl.BlockSpec((tm, tc), lambda jc, i: (i, jc)),
        out_shape=jax.ShapeDtypeStruct((m, WIDTH), BF16),
        scratch_shapes=[pltpu.VMEM((8, tc), F32)],
        compiler_params=pltpu.CompilerParams(
            dimension_semantics=("parallel", "arbitrary"), vmem_limit_bytes=VMEM_LIMIT),
        name="conv_proj",
    )(h, w_all, w_all, w_all, w_all, conv_w, conv_b.reshape(1, WIDTH))


def _attn_kernel(qn_ref, kn_ref, vn_ref, zn_ref, qc_ref, kc_ref, vc_ref, o_ref,
                 o_dil, lse_dil, o_far, lse_far):
    seq = qn_ref.shape[1]
    t = ATTN_TILE
    e = HEAD_DIM
    n_cls = qc_ref.shape[0]
    row = lax.broadcasted_iota(jnp.int32, (t, t), 0)
    col = lax.broadcasted_iota(jnp.int32, (t, t), 1)

    def bdot(a, b, contract_b):
        return lax.dot_general(a, b, (((2,), (contract_b,)), ((0,), (0,))),
                               preferred_element_type=F32)

    def partials(q, k_cur, k_prev, v_cur, v_prev, cur_ok, prev_ok):
        g = q.shape[0]
        keys = jnp.concatenate([k_prev, k_cur], axis=1)
        vals = jnp.concatenate([v_prev, v_cur], axis=1)
        ok = jnp.concatenate([jnp.broadcast_to(prev_ok, (g, t, t)),
                              jnp.broadcast_to(cur_ok, (g, t, t))], axis=2)
        s = jnp.where(ok, bdot(q, keys, 2), NEG_BIG)
        m = jnp.max(s, axis=-1, keepdims=True)
        p = jnp.exp2(s - m).astype(BF16)
        acc_l = bdot(p, jnp.concatenate([vals, jnp.ones(vals.shape, BF16)], axis=2), 1)
        return acc_l[:, :, :e], m, acc_l[:, :, e:]

    def previous_tiles(x):
        return jnp.concatenate([x[:, :1], x[:, :-1]], axis=1)

    def dilated_pass(dil, finish):
        step = dil // n_cls
        tiles = seq // (dil * t)
        classes = min(ATTN_BATCH[dil] // tiles, dil)
        g = classes * tiles
        slot = lax.broadcasted_iota(jnp.int32, (g, t, t), 0)
        cur_ok = col <= row
        prev_ok = jnp.logical_and(col >= row, slot % tiles > 0)

        def gather(ref, r0):
            return jnp.stack([load_class(ref, r0 + c, step, tiles * t).astype(BF16).reshape(tiles, t, e)
                              for c in range(classes)], axis=0)

        def body(i, carry):
            r0 = i * classes
            k, v = gather(kc_ref, r0), gather(vc_ref, r0)
            acc, m, l = partials(gather(qc_ref, r0).reshape(g, t, e),
                                 k.reshape(g, t, e), previous_tiles(k).reshape(g, t, e),
                                 v.reshape(g, t, e), previous_tiles(v).reshape(g, t, e),
                                 cur_ok, prev_ok)
            m = jnp.broadcast_to(m, acc.shape)
            for c in range(classes):
                part = lambda x: x[c * tiles:(c + 1) * tiles].reshape(tiles * t, e)
                finish(r0 + c, part(acc), part(m), part(l))
            return carry

        lax.fori_loop(0, dil // classes, body, 0)

    def load_class(ref, r, step, n):
        if step == 1:
            return ref[r]
        return ref[r % n_cls, pl.ds(r // n_cls, n, stride=step), :]

    def store_class(ref, r, step, n, value):
        if step == 1:
            ref[r] = value
        else:
            ref[r % n_cls, pl.ds(r // n_cls, n, stride=step), :] = value

    far = DILATIONS[2]
    near = DILATIONS[1]
    assert near == n_cls

    def finish_far(r, acc, m, l):
        n = acc.shape[0]
        store_class(o_far, r, far // n_cls, n, acc / l)
        store_class(lse_far, r, far // n_cls, n, m + jnp.log2(l))

    def finish_near(r, acc, m, l):
        lse_f = lse_far[r]
        top = jnp.maximum(m, lse_f)
        w_n = jnp.exp2(m - top)
        w_f = jnp.exp2(lse_f - top)
        den = w_n * l + w_f
        dst = pl.ds(r, acc.shape[0], stride=near)
        o_dil[dst, :] = (w_n * acc + w_f * o_far[r]) / den
        lse_dil[dst, :] = top + jnp.log2(den)

    dilated_pass(far, finish_far)
    dilated_pass(near, finish_near)

    g = min(ATTN_BATCH[1], seq // t)
    slot = lax.broadcasted_iota(jnp.int32, (g, t, t), 0)
    for i in range(seq // (g * t)):
        start = i * g * t
        rows = slice(start, start + g * t)
        if i == 0:
            with_prev = lambda ref: jnp.concatenate([ref[0, :t], ref[0, rows]], axis=0)
            prev_ok = jnp.logical_and(col >= row, slot > 0)
        else:
            with_prev = lambda ref: ref[0, start - t:start + g * t]
            prev_ok = col >= row
        k_all, v_all = with_prev(kn_ref), with_prev(vn_ref)
        acc1, m1, l1 = partials(qn_ref[0, rows].reshape(g, t, e),
                                k_all[t:].reshape(g, t, e), k_all[:g * t].reshape(g, t, e),
                                v_all[t:].reshape(g, t, e), v_all[:g * t].reshape(g, t, e),
                                col <= row, prev_ok)
        load = lambda ref: ref[rows, :].reshape(g, t, e)
        lse_d = load(lse_dil)
        m = jnp.maximum(m1, lse_d)
        w1 = jnp.exp2(m1 - m)
        w_d = jnp.exp2(lse_d - m)
        num = w1 * acc1 + w_d * load(o_dil)
        den = w1 * l1 + w_d
        z = zn_ref[0, rows].astype(F32).reshape(g, t, e)
        o_ref[rows, :] = (num / den * _silu(z)).astype(BF16).reshape(g * t, e)


def _attention(nat, cm, batch, seq):
    m = nat.shape[1]

    def spec(base):
        return pl.BlockSpec((1, seq, LANE), lambda b, h: (base + h, b, 0))

    def cm_spec(base):
        return pl.BlockSpec((None, None, N_CLASSES, seq // N_CLASSES, LANE),
                            lambda b, h: (base + h, b, 0, 0, 0))

    return pl.pallas_call(
        _attn_kernel,
        grid=(batch, N_HEADS),
        in_specs=[spec(SLAB_QA), spec(SLAB_KA), spec(SLAB_VA), spec(SLAB_ZA),
                  cm_spec(SLAB_QA), cm_spec(SLAB_KA), cm_spec(SLAB_VA)],
        out_specs=pl.BlockSpec((seq, LANE), lambda b, h: (b, h)),
        out_shape=jax.ShapeDtypeStruct((m, WIDTH), BF16),
        scratch_shapes=([pltpu.VMEM((seq, LANE), F32)] * 2
                        + [pltpu.VMEM((N_CLASSES, seq // N_CLASSES, LANE), F32)] * 2),
        compiler_params=pltpu.CompilerParams(
            dimension_semantics=("parallel", "parallel"), vmem_limit_bytes=VMEM_LIMIT),
        name="dilated_attn",
    )(nat, nat, nat, nat, cm, cm, cm)


def _ret_kernel(lg_ref, q_ref, k_ref, v_ref, z_ref, cos_ref, sin_ref, o_ref):
    seq = q_ref.shape[1]
    c = RET_CHUNK
    lg = lg_ref[pl.program_id(1)]
    row = lax.broadcasted_iota(jnp.int32, (c, c), 0).astype(F32)
    col = lax.broadcasted_iota(jnp.int32, (c, c), 1).astype(F32)
    rel = row - col
    key_scale = HEAD_DIM ** -0.5
    decay_mask = jnp.where(rel >= 0, jnp.exp(jnp.maximum(rel, 0.0) * lg), 0.0) * key_scale
    q_decay = jnp.exp((row + 1.0) * lg)
    k_decay = jnp.exp((c - 1.0 - row) * lg) * key_scale
    chunk_decay = jnp.exp(jnp.full((c, c), float(c), F32) * lg)

    g = min(RET_BATCH, seq // c)
    e = HEAD_DIM

    def bdot(a, b, contract_a, contract_b):
        return lax.dot_general(a, b, (((contract_a,), (contract_b,)), ((0,), (0,))),
                               preferred_element_type=F32)

    def body(i, state):
        sl = pl.ds(pl.multiple_of(i * (g * c), g * c), g * c)
        cos, sin = cos_ref[sl, :], sin_ref[sl, :]
        rotate = lambda x: x * cos + pltpu.roll(x, e // 2, 1) * sin
        q = rotate(q_ref[0, sl, :].astype(F32)).reshape(g, c, e)
        k = rotate(k_ref[0, sl, :].astype(F32)).reshape(g, c, e)
        v = v_ref[0, sl, :].reshape(g, c, e)
        inner = bdot(q.astype(BF16), k.astype(BF16), 2, 2) * decay_mask
        kv = bdot((k * k_decay).astype(BF16), v, 1, 1)
        states = []
        for j in range(g):
            states.append(state)
            state = state * chunk_decay + kv[j]
        lhs = jnp.concatenate([inner.astype(BF16), (q * q_decay).astype(BF16)], axis=2)
        rhs = jnp.concatenate([v, jnp.stack(states).astype(BF16)], axis=1)
        o = bdot(lhs, rhs, 2, 1)
        o = o * lax.rsqrt(jnp.mean(o * o, axis=-1, keepdims=True) + NORM_EPS)
        z = z_ref[0, sl, :].astype(F32).reshape(g, c, e)
        o_ref[sl, :] = (o * _silu(z)).astype(BF16).reshape(g * c, e)
        return state

    lax.fori_loop(0, seq // (g * c), body, jnp.zeros((e, e), F32))


def _retention(proj, log_gamma, cos2, sin2, batch, seq):
    m = proj.shape[1]

    def spec(base):
        return pl.BlockSpec((1, seq, LANE), lambda b, h, lg: (base + h, b, 0))

    table = pl.BlockSpec((seq, LANE), lambda b, h, lg: (0, 0))
    return pl.pallas_call(
        _ret_kernel,
        grid_spec=pltpu.PrefetchScalarGridSpec(
            num_scalar_prefetch=1,
            grid=(batch, N_HEADS),
            in_specs=[spec(SLAB_QR), spec(SLAB_KR), spec(SLAB_VR), spec(SLAB_ZR), table, table],
            out_specs=pl.BlockSpec((seq, LANE), lambda b, h, lg: (b, h)),
        ),
        out_shape=jax.ShapeDtypeStruct((m, WIDTH), BF16),
        compiler_params=pltpu.CompilerParams(
            dimension_semantics=("parallel", "parallel"), vmem_limit_bytes=VMEM_LIMIT),
        name="retention",
    )(log_gamma, proj, proj, proj, proj, cos2, sin2)


def _merge_kernel(oa_ref, oc_ref, or_ref, ga_ref, gc_ref, gr_ref, wa_ref, wc_ref, wr_ref, o_ref):
    ya = _dot(oa_ref[...], wa_ref[...].astype(BF16))
    yc = _dot(oc_ref[...], wc_ref[...].astype(BF16))
    yr = _dot(or_ref[...], wr_ref[...].astype(BF16))
    for s in range(ga_ref.shape[0]):
        lanes = slice(s * LANE, (s + 1) * LANE)
        merged = (_sigmoid(ga_ref[s].astype(F32)) * ya[:, lanes]
                  + _sigmoid(gc_ref[s].astype(F32)) * yc[:, lanes]
                  + _sigmoid(gr_ref[s].astype(F32)) * yr[:, lanes])
        o_ref[:, lanes] = merged.astype(BF16)


def _merge(proj, oa, oc, orr, wa, wc, wr, layer, tm=1024, tn=512):
    m = proj.shape[1]
    gs = tn // LANE

    def gate(base):
        return pl.BlockSpec((gs, tm, LANE), lambda n, i: (base // gs + n, i, 0))

    wspec = pl.BlockSpec((None, WIDTH, tn), lambda n, i: (layer, 0, n))
    own = pl.BlockSpec((tm, WIDTH), lambda n, i: (i, 0))
    return pl.pallas_call(
        _merge_kernel,
        grid=(D_MODEL // tn, m // tm),
        in_specs=[own, own, own, gate(SLAB_GA), gate(SLAB_GC), gate(SLAB_GR), wspec, wspec, wspec],
        out_specs=pl.BlockSpec((tm, tn), lambda n, i: (i, n)),
        out_shape=jax.ShapeDtypeStruct((m, D_MODEL), BF16),
        compiler_params=pltpu.CompilerParams(
            dimension_semantics=("parallel", "parallel"), vmem_limit_bytes=VMEM_LIMIT),
        name="branch_merge",
    )(oa, oc, orr, proj, proj, proj, wa, wc, wr)


def _out_proj_kernel(m_ref, w_ref, x_ref, g_ref, *rest):
    y = _dot(m_ref[...], w_ref[...].astype(BF16))
    ms = jnp.mean(y * y, axis=-1, keepdims=True)
    x_new = x_ref[...] + y * lax.rsqrt(ms + NORM_EPS) * g_ref[...]
    if len(rest) == 1:
        (o_ref,) = rest
    else:
        next_g_ref, o_ref, h_ref = rest
        ms = jnp.mean(x_new * x_new, axis=-1, keepdims=True)
        h_ref[...] = (x_new * lax.rsqrt(ms + NORM_EPS) * next_g_ref[...]).astype(BF16)
    o_ref[...] = x_new


def _out_proj(merged, w_all, layer, x2, g, next_g=None, tm=512):
    m, d = x2.shape
    rows = pl.BlockSpec((tm, d), lambda i: (i, 0))
    gain = pl.BlockSpec((1, d), lambda i: (0, 0))
    more = next_g is not None
    out = pl.pallas_call(
        _out_proj_kernel,
        grid=(m // tm,),
        in_specs=[rows,
                  pl.BlockSpec((None, d, d), lambda i: (layer, 0, 0), pipeline_mode=pl.Buffered(1)),
                  rows, gain] + [gain] * more,
        out_specs=[rows] + [rows] * more,
        out_shape=[jax.ShapeDtypeStruct((m, d), F32)] + [jax.ShapeDtypeStruct((m, d), BF16)] * more,
        compiler_params=pltpu.CompilerParams(
            dimension_semantics=("parallel",), vmem_limit_bytes=VMEM_LIMIT),
        name="out_proj",
    )(merged, w_all, x2, g.reshape(1, d), *([next_g.reshape(1, d)] if more else []))
    return (out[0], out[1]) if more else (out[0], None)


def _rotary_tables(seq):
    half = HEAD_DIM // 2
    inv_freq = ROPE_BASE ** (-jnp.arange(half, dtype=F32) / half)
    ang = jnp.arange(seq, dtype=F32)[:, None] * inv_freq[None, :]
    cos, sin = jnp.cos(ang), jnp.sin(ang)
    return jnp.concatenate([cos, cos], axis=-1), jnp.concatenate([-sin, sin], axis=-1)


def kernel(x, pre_norm_g, post_norm_g, w_in, conv_w, conv_b, w_branch_a, w_branch_c, w_branch_r, w_out):
    batch, seq, d = x.shape
    assert d == D_MODEL and w_in.shape[-1] == N_IN
    assert seq % (max(DILATIONS) * ATTN_TILE) == 0 and seq % RET_CHUNK == 0
    cos2, sin2 = _rotary_tables(seq)
    log_gamma = jnp.log1p(-jnp.exp2(-5.0 - jnp.arange(N_HEADS, dtype=F32)))
    x2 = x.reshape(batch * seq, d)
    depth = w_in.shape[0]
    h = _pre_norm(x2, pre_norm_g[0])
    for layer in range(depth):
        attn_nat, attn_cm = _attn_proj(h, w_in, layer, batch, seq)
        oa = _attention(attn_nat, attn_cm, batch, seq)
        proj = _in_proj(h, w_in, layer, COL_QR, N_IN // LANE - COL_QR)
        orr = _retention(proj, log_gamma, cos2, sin2, batch, seq)
        oc = _conv_proj(h, w_in, conv_w[layer], conv_b[layer], layer, seq)
        merged = _merge(proj, oa, oc, orr, w_branch_a, w_branch_c, w_branch_r, layer)
        next_g = pre_norm_g[layer + 1] if layer + 1 < depth else None
        x2, h = _out_proj(merged, w_out, layer, x2, post_norm_g[layer], next_g)
    return x2.reshape(batch, seq, d)
```

```python
import functools
import math

import jax
import jax.numpy as jnp
from jax import lax
from jax.experimental import pallas as pl
from jax.experimental.pallas import tpu as pltpu

F32 = jnp.float32
BF16 = jnp.bfloat16

LANE = 128
D_MODEL = 2048
HEAD_DIM = 128
N_HEADS = 12
WIDTH = N_HEADS * HEAD_DIM
N_IN = 12 * WIDTH + 3 * D_MODEL
DILATIONS = (1, 4, 16)
ATTN_TILE = 128
ATTN_BATCH = {1: 32, 4: 32, 16: 32}
LOG2_E = math.log2(math.e)
RET_CHUNK = 128
RET_BATCH = 32
CONV_K = 3
ROPE_BASE = 10000.0
NORM_EPS = 1e-6
NEG_BIG = -1e30

COL_QA, COL_KA, COL_VA, COL_ZA = 0, 12, 24, 36
COL_UC, COL_BC, COL_CC, COL_ZC = 48, 60, 72, 84
COL_QR, COL_KR, COL_VR, COL_ZR = 96, 108, 120, 132
COL_GA, COL_GC, COL_GR = 144, 160, 176
ATTN_SLABS = COL_ZA - COL_QA
SLAB_QA, SLAB_KA, SLAB_VA = COL_QA, COL_KA, COL_VA
REST_SEGMENTS = ((COL_ZA, COL_UC - COL_ZA), (COL_QR, N_IN // LANE - COL_QR))
SLAB_ZA = 0
_SHIFT = COL_QR - REST_SEGMENTS[0][1]
SLAB_QR, SLAB_KR, SLAB_VR, SLAB_ZR = (c - _SHIFT for c in (COL_QR, COL_KR, COL_VR, COL_ZR))
SLAB_GA, SLAB_GC, SLAB_GR = (c - _SHIFT for c in (COL_GA, COL_GC, COL_GR))
N_CLASSES = 4

VMEM_LIMIT = 56 * 1024 * 1024


def _sigmoid(x):
    return 0.5 * jnp.tanh(0.5 * x) + 0.5


def _silu(x):
    return x * _sigmoid(x)


def _dot(a, b):
    return jnp.dot(a, b, preferred_element_type=F32)


def _pre_norm_kernel(x_ref, g_ref, o_ref):
    xf = x_ref[...]
    ms = jnp.mean(xf * xf, axis=-1, keepdims=True)
    o_ref[...] = (xf * lax.rsqrt(ms + NORM_EPS) * g_ref[...]).astype(BF16)


def _pre_norm(x2, g, tm=512):
    m, d = x2.shape
    return pl.pallas_call(
        _pre_norm_kernel,
        grid=(m // tm,),
        in_specs=[pl.BlockSpec((tm, d), lambda i: (i, 0)), pl.BlockSpec((1, d), lambda i: (0, 0))],
        out_specs=pl.BlockSpec((tm, d), lambda i: (i, 0)),
        out_shape=jax.ShapeDtypeStruct((m, d), BF16),
        compiler_params=pltpu.CompilerParams(
            dimension_semantics=("parallel",), vmem_limit_bytes=VMEM_LIMIT),
        name="pre_norm",
    )(x2, g.reshape(1, d))


def _in_proj_kernel(h_ref, w_ref, o_ref):
    acc = _dot(h_ref[...], w_ref[...].astype(BF16))
    for c in range(o_ref.shape[0]):
        o_ref[c] = acc[:, c * LANE:(c + 1) * LANE].astype(BF16)


def _in_proj(h, w_all, layer, segments, tm=2048, tn=768):
    m, d = h.shape
    (col_a, slabs_a), (col_b, slabs_b) = segments
    tiles_a = slabs_a * LANE // tn
    n_slabs = slabs_a + slabs_b

    def w_tile(j):
        return jnp.where(j < tiles_a, col_a * LANE // tn + j, col_b * LANE // tn + j - tiles_a)

    return pl.pallas_call(
        _in_proj_kernel,
        grid=(n_slabs * LANE // tn, m // tm),
        in_specs=[
            pl.BlockSpec((tm, d), lambda j, i: (i, 0)),
            pl.BlockSpec((None, d, tn), lambda j, i: (layer, 0, w_tile(j))),
        ],
        out_specs=pl.BlockSpec((tn // LANE, tm, LANE), lambda j, i: (j, i, 0)),
        out_shape=jax.ShapeDtypeStruct((n_slabs, m, LANE), BF16),
        compiler_params=pltpu.CompilerParams(
            dimension_semantics=("parallel", "parallel"), vmem_limit_bytes=VMEM_LIMIT),
        name="in_proj",
    )(h, w_all)


def _attn_proj_kernel(h_ref, w_ref, nat_ref, cm_ref, scr_ref):
    j = pl.program_id(0)
    n_slabs, n_cls, rows = cm_ref.shape[0], cm_ref.shape[1], cm_ref.shape[2]
    pair = 2 * LANE
    for c0 in range(0, n_slabs, 2):
        acc = _dot(h_ref[...], w_ref[:, c0 * LANE:c0 * LANE + pair].astype(BF16))
        for c in (c0, c0 + 1):
            is_q = (j * n_slabs + c) < N_HEADS
            slab = (acc[:, (c - c0) * LANE:(c - c0 + 1) * LANE]
                    * jnp.where(is_q, HEAD_DIM ** -0.5 * LOG2_E, 1.0))
            nat_ref[c] = slab.astype(BF16)
            scr_ref[c] = slab
            for r in range(n_cls):
                cm_ref[c, r] = scr_ref[c, pl.ds(r, rows, stride=n_cls), :]


def _attn_proj(h, w_all, layer, batch, seq, tm=1024, tn=768):
    m, d = h.shape
    tiles_per_seq = seq // tm
    slabs = tn // LANE
    return pl.pallas_call(
        _attn_proj_kernel,
        grid=(ATTN_SLABS * LANE // tn, m // tm),
        in_specs=[
            pl.BlockSpec((tm, d), lambda j, i: (i, 0)),
            pl.BlockSpec((None, d, tn), lambda j, i: (layer, 0, COL_QA * LANE // tn + j)),
        ],
        out_specs=[
            pl.BlockSpec((slabs, tm, LANE), lambda j, i: (j, i, 0)),
            pl.BlockSpec((slabs, None, N_CLASSES, tm // N_CLASSES, LANE),
                         lambda j, i: (j, i // tiles_per_seq, 0, i % tiles_per_seq, 0)),
        ],
        out_shape=[jax.ShapeDtypeStruct((ATTN_SLABS, m, LANE), BF16),
                   jax.ShapeDtypeStruct((ATTN_SLABS, batch, N_CLASSES, seq // N_CLASSES, LANE), F32)],
        scratch_shapes=[pltpu.VMEM((slabs, tm, LANE), F32)],
        compiler_params=pltpu.CompilerParams(
            dimension_semantics=("parallel", "parallel"), vmem_limit_bytes=VMEM_LIMIT),
        name="attn_proj",
    )(h, w_all)


def _conv_proj_kernel(h_ref, wu_ref, wb_ref, wc_ref, wz_ref, cw_ref, cb_ref, o_ref, halo_ref,
                      *, tiles_per_seq):
    i = pl.program_id(1)
    tm = h_ref.shape[0]

    @pl.when(i == 0)
    def _():
        halo_ref[...] = jnp.zeros(halo_ref.shape, F32)

    h = h_ref[...]
    cu = _dot(h, wc_ref[...].astype(BF16)) * _dot(h, wu_ref[...].astype(BF16))
    halo = jnp.where((i % tiles_per_seq) == 0, 0.0, halo_ref[...])
    halo_ref[...] = cu[tm - 8:, :]
    row = lax.broadcasted_iota(jnp.int32, (8, cu.shape[1]), 0)
    back1 = pltpu.roll(cu, 1, 0)
    back2 = pltpu.roll(cu, 2, 0)
    head1 = jnp.where(row == 0, halo[7:8, :], back1[:8])
    head2 = jnp.where(row == 0, halo[6:7, :], jnp.where(row == 1, halo[7:8, :], back2[:8]))
    back1 = jnp.concatenate([head1, back1[8:]], axis=0)
    back2 = jnp.concatenate([head2, back2[8:]], axis=0)
    conv = (cb_ref[...] + cw_ref[2:3, :] * cu + cw_ref[0:1, :] * back2 + cw_ref[1:2, :] * back1)
    hz = 0.5 * _dot(h, wz_ref[...].astype(BF16))
    gated = (conv * (_dot(h, wb_ref[...].astype(BF16)) * hz)) * (1.0 + jnp.tanh(hz))
    o_ref[...] = gated.astype(BF16)


def _conv_proj(h, w_all, conv_w, conv_b, layer, seq, tm=2048, tc=256):
    m, d = h.shape

    def wspec(col):
        return pl.BlockSpec((None, d, tc), lambda jc, i: (layer, 0, col * LANE // tc + jc))

    return pl.pallas_call(
        functools.partial(_conv_proj_kernel, tiles_per_seq=seq // tm),
        grid=(WIDTH // tc, m // tm),
        in_specs=[pl.BlockSpec((tm, d), lambda jc, i: (i, 0)),
                  wspec(COL_UC), wspec(COL_BC), wspec(COL_CC), wspec(COL_ZC),
                  pl.BlockSpec((CONV_K, tc), lambda jc, i: (0, jc)),
                  pl.BlockSpec((1, tc), lambda jc, i: (0, jc))],
        out_specs=pl.BlockSpec((tm, tc), lambda jc, i: (i, jc)),
        out_shape=jax.ShapeDtypeStruct((m, WIDTH), BF16),
        scratch_shapes=[pltpu.VMEM((8, tc), F32)],
        compiler_params=pltpu.CompilerParams(
            dimension_semantics=("parallel", "arbitrary"), vmem_limit_bytes=VMEM_LIMIT),
        name="conv_proj",
    )(h, w_all, w_all, w_all, w_all, conv_w, conv_b.reshape(1, WIDTH))


def _attn_kernel(qn_ref, kn_ref, vn_ref, zn_ref, qc_ref, kc_ref, vc_ref, o_ref,
                 acc_dil, m_dil, l_dil, acc_far, m_far, l_far):
    seq = qn_ref.shape[1]
    t = ATTN_TILE
    e = HEAD_DIM
    n_cls = qc_ref.shape[0]
    row = lax.broadcasted_iota(jnp.int32, (t, t), 0)
    col = lax.broadcasted_iota(jnp.int32, (t, t), 1)

    def bdot(a, b, contract_b):
        return lax.dot_general(a, b, (((2,), (contract_b,)), ((0,), (0,))),
                               preferred_element_type=F32)

    def partials(q, k_cur, k_prev, v_cur, v_prev, cur_ok, prev_ok):
        g = q.shape[0]
        keys = jnp.concatenate([k_prev, k_cur], axis=1)
        vals = jnp.concatenate([v_prev, v_cur], axis=1)
        ok = jnp.concatenate([jnp.broadcast_to(prev_ok, (g, t, t)),
                              jnp.broadcast_to(cur_ok, (g, t, t))], axis=2)
        s = jnp.where(ok, bdot(q, keys, 2), NEG_BIG)
        m = jnp.max(s, axis=-1, keepdims=True)
        p = jnp.exp2(s - m).astype(BF16)
        acc_l = bdot(p, jnp.concatenate([vals, jnp.ones(vals.shape, BF16)], axis=2), 1)
        return acc_l[:, :, :e], m, acc_l[:, :, e:]

    def previous_tiles(x):
        return jnp.concatenate([x[:, :1], x[:, :-1]], axis=1)

    def dilated_pass(dil, finish):
        step = dil // n_cls
        tiles = seq // (dil * t)
        classes = min(ATTN_BATCH[dil] // tiles, dil)
        g = classes * tiles
        slot = lax.broadcasted_iota(jnp.int32, (g, t, t), 0)
        cur_ok = col <= row
        prev_ok = jnp.logical_and(col >= row, slot % tiles > 0)

        def gather(ref, r0):
            return jnp.stack([load_class(ref, r0 + c, step, tiles * t).astype(BF16).reshape(tiles, t, e)
                              for c in range(classes)], axis=0)

        def body(i, carry):
            r0 = i * classes
            k, v = gather(kc_ref, r0), gather(vc_ref, r0)
            acc, m, l = partials(gather(qc_ref, r0).reshape(g, t, e),
                                 k.reshape(g, t, e), previous_tiles(k).reshape(g, t, e),
                                 v.reshape(g, t, e), previous_tiles(v).reshape(g, t, e),
                                 cur_ok, prev_ok)
            m = jnp.broadcast_to(m, acc.shape)
            for c in range(classes):
                part = lambda x: x[c * tiles:(c + 1) * tiles].reshape(tiles * t, e)
                finish(r0 + c, part(acc), part(m), part(l))
            return carry

        lax.fori_loop(0, dil // classes, body, 0)

    def load_class(ref, r, step, n):
        if step == 1:
            return ref[r]
        return ref[r % n_cls, pl.ds(r // n_cls, n, stride=step), :]

    def store_class(ref, r, step, n, value):
        if step == 1:
            ref[r] = value
        else:
            ref[r % n_cls, pl.ds(r // n_cls, n, stride=step), :] = value

    far = DILATIONS[2]
    near = DILATIONS[1]
    assert near == n_cls

    def finish_far(r, acc, m, l):
        n = acc.shape[0]
        for ref, value in ((acc_far, acc), (m_far, m), (l_far, l)):
            store_class(ref, r, far // n_cls, n, value)

    def finish_near(r, acc, m, l):
        top = jnp.maximum(m, m_far[r])
        w_n = jnp.exp2(m - top)
        w_f = jnp.exp2(m_far[r] - top)
        dst = pl.ds(r, acc.shape[0], stride=near)
        acc_dil[dst, :] = w_n * acc + w_f * acc_far[r]
        m_dil[dst, :] = top
        l_dil[dst, :] = w_n * l + w_f * l_far[r]

    dilated_pass(far, finish_far)
    dilated_pass(near, finish_near)

    g = min(ATTN_BATCH[1], seq // t)
    slot = lax.broadcasted_iota(jnp.int32, (g, t, t), 0)
    for i in range(seq // (g * t)):
        start = i * g * t
        rows = slice(start, start + g * t)
        if i == 0:
            with_prev = lambda ref: jnp.concatenate([ref[0, :t], ref[0, rows]], axis=0)
            prev_ok = jnp.logical_and(col >= row, slot > 0)
        else:
            with_prev = lambda ref: ref[0, start - t:start + g * t]
            prev_ok = col >= row
        k_all, v_all = with_prev(kn_ref), with_prev(vn_ref)
        acc1, m1, l1 = partials(qn_ref[0, rows].reshape(g, t, e),
                                k_all[t:].reshape(g, t, e), k_all[:g * t].reshape(g, t, e),
                                v_all[t:].reshape(g, t, e), v_all[:g * t].reshape(g, t, e),
                                col <= row, prev_ok)
        load = lambda ref: ref[rows, :].reshape(g, t, e)
        m_d = load(m_dil)
        m = jnp.maximum(m1, m_d)
        w1 = jnp.exp2(m1 - m)
        w_d = jnp.exp2(m_d - m)
        num = w1 * acc1 + w_d * load(acc_dil)
        den = w1 * l1 + w_d * load(l_dil)
        z = zn_ref[0, rows].astype(F32).reshape(g, t, e)
        o_ref[rows, :] = (num / den * _silu(z)).astype(BF16).reshape(g * t, e)


def _attention(nat, cm, proj, batch, seq):
    m = nat.shape[1]

    def spec(base):
        return pl.BlockSpec((1, seq, LANE), lambda b, h: (base + h, b, 0))

    def cm_spec(base):
        return pl.BlockSpec((None, None, N_CLASSES, seq // N_CLASSES, LANE),
                            lambda b, h: (base + h, b, 0, 0, 0))

    return pl.pallas_call(
        _attn_kernel,
        grid=(batch, N_HEADS),
        in_specs=[spec(SLAB_QA), spec(SLAB_KA), spec(SLAB_VA), spec(SLAB_ZA),
                  cm_spec(SLAB_QA), cm_spec(SLAB_KA), cm_spec(SLAB_VA)],
        out_specs=pl.BlockSpec((seq, LANE), lambda b, h: (b, h)),
        out_shape=jax.ShapeDtypeStruct((m, WIDTH), BF16),
        scratch_shapes=([pltpu.VMEM((seq, LANE), F32)] * 3
                        + [pltpu.VMEM((N_CLASSES, seq // N_CLASSES, LANE), F32)] * 3),
        compiler_params=pltpu.CompilerParams(
            dimension_semantics=("parallel", "parallel"), vmem_limit_bytes=VMEM_LIMIT),
        name="dilated_attn",
    )(nat, nat, nat, proj, cm, cm, cm)


def _ret_kernel(lg_ref, q_ref, k_ref, v_ref, z_ref, cos_ref, sin_ref, o_ref):
    seq = q_ref.shape[1]
    c = RET_CHUNK
    lg = lg_ref[pl.program_id(1)]
    row = lax.broadcasted_iota(jnp.int32, (c, c), 0).astype(F32)
    col = lax.broadcasted_iota(jnp.int32, (c, c), 1).astype(F32)
    rel = row - col
    key_scale = HEAD_DIM ** -0.5
    decay_mask = jnp.where(rel >= 0, jnp.exp(jnp.maximum(rel, 0.0) * lg), 0.0) * key_scale
    q_decay = jnp.exp((row + 1.0) * lg)
    k_decay = jnp.exp((c - 1.0 - row) * lg) * key_scale
    chunk_decay = jnp.exp(jnp.full((c, c), float(c), F32) * lg)

    g = min(RET_BATCH, seq // c)
    e = HEAD_DIM

    def bdot(a, b, contract_a, contract_b):
        return lax.dot_general(a, b, (((contract_a,), (contract_b,)), ((0,), (0,))),
                               preferred_element_type=F32)

    def body(i, state):
        sl = pl.ds(pl.multiple_of(i * (g * c), g * c), g * c)
        cos, sin = cos_ref[sl, :], sin_ref[sl, :]
        rotate = lambda x: x * cos + pltpu.roll(x, e // 2, 1) * sin
        q = rotate(q_ref[0, sl, :].astype(F32)).reshape(g, c, e)
        k = rotate(k_ref[0, sl, :].astype(F32)).reshape(g, c, e)
        v = v_ref[0, sl, :].reshape(g, c, e)
        inner = bdot(q.astype(BF16), k.astype(BF16), 2, 2) * decay_mask
        kv = bdot((k * k_decay).astype(BF16), v, 1, 1)
        states = []
        for j in range(g):
            states.append(state)
            state = state * chunk_decay + kv[j]
        lhs = jnp.concatenate([inner.astype(BF16), (q * q_decay).astype(BF16)], axis=2)
        rhs = jnp.concatenate([v, jnp.stack(states).astype(BF16)], axis=1)
        o = bdot(lhs, rhs, 2, 1)
        o = o * lax.rsqrt(jnp.mean(o * o, axis=-1, keepdims=True) + NORM_EPS)
        z = z_ref[0, sl, :].astype(F32).reshape(g, c, e)
        o_ref[sl, :] = (o * _silu(z)).astype(BF16).reshape(g * c, e)
        return state

    lax.fori_loop(0, seq // (g * c), body, jnp.zeros((e, e), F32))


def _retention(proj, log_gamma, cos2, sin2, batch, seq):
    m = proj.shape[1]

    def spec(base):
        return pl.BlockSpec((1, seq, LANE), lambda b, h, lg: (base + h, b, 0))

    table = pl.BlockSpec((seq, LANE), lambda b, h, lg: (0, 0))
    return pl.pallas_call(
        _ret_kernel,
        grid_spec=pltpu.PrefetchScalarGridSpec(
            num_scalar_prefetch=1,
            grid=(batch, N_HEADS),
            in_specs=[spec(SLAB_QR), spec(SLAB_KR), spec(SLAB_VR), spec(SLAB_ZR), table, table],
            out_specs=pl.BlockSpec((seq, LANE), lambda b, h, lg: (b, h)),
        ),
        out_shape=jax.ShapeDtypeStruct((m, WIDTH), BF16),
        compiler_params=pltpu.CompilerParams(
            dimension_semantics=("parallel", "parallel"), vmem_limit_bytes=VMEM_LIMIT),
        name="retention",
    )(log_gamma, proj, proj, proj, proj, cos2, sin2)


def _merge_kernel(oa_ref, oc_ref, or_ref, ga_ref, gc_ref, gr_ref, wa_ref, wc_ref, wr_ref, o_ref):
    ya = _dot(oa_ref[...], wa_ref[...].astype(BF16))
    yc = _dot(oc_ref[...], wc_ref[...].astype(BF16))
    yr = _dot(or_ref[...], wr_ref[...].astype(BF16))
    for s in range(ga_ref.shape[0]):
        lanes = slice(s * LANE, (s + 1) * LANE)
        merged = (_sigmoid(ga_ref[s].astype(F32)) * ya[:, lanes]
                  + _sigmoid(gc_ref[s].astype(F32)) * yc[:, lanes]
                  + _sigmoid(gr_ref[s].astype(F32)) * yr[:, lanes])
        o_ref[:, lanes] = merged.astype(BF16)


def _merge(proj, oa, oc, orr, wa, wc, wr, layer, tm=1024, tn=512):
    m = proj.shape[1]
    gs = tn // LANE

    def gate(base):
        return pl.BlockSpec((gs, tm, LANE), lambda n, i: (base // gs + n, i, 0))

    wspec = pl.BlockSpec((None, WIDTH, tn), lambda n, i: (layer, 0, n))
    own = pl.BlockSpec((tm, WIDTH), lambda n, i: (i, 0))
    return pl.pallas_call(
        _merge_kernel,
        grid=(D_MODEL // tn, m // tm),
        in_specs=[own, own, own, gate(SLAB_GA), gate(SLAB_GC), gate(SLAB_GR), wspec, wspec, wspec],
        out_specs=pl.BlockSpec((tm, tn), lambda n, i: (i, n)),
        out_shape=jax.ShapeDtypeStruct((m, D_MODEL), BF16),
        compiler_params=pltpu.CompilerParams(
            dimension_semantics=("parallel", "parallel"), vmem_limit_bytes=VMEM_LIMIT),
        name="branch_merge",
    )(oa, oc, orr, proj, proj, proj, wa, wc, wr)


def _out_proj_kernel(m_ref, w_ref, x_ref, g_ref, *rest):
    y = _dot(m_ref[...], w_ref[...].astype(BF16))
    ms = jnp.mean(y * y, axis=-1, keepdims=True)
    x_new = x_ref[...] + y * lax.rsqrt(ms + NORM_EPS) * g_ref[...]
    if len(rest) == 1:
        (o_ref,) = rest
    else:
        next_g_ref, o_ref, h_ref = rest
        ms = jnp.mean(x_new * x_new, axis=-1, keepdims=True)
        h_ref[...] = (x_new * lax.rsqrt(ms + NORM_EPS) * next_g_ref[...]).astype(BF16)
    o_ref[...] = x_new


def _out_proj(merged, w_all, layer, x2, g, next_g=None, tm=512):
    m, d = x2.shape
    rows = pl.BlockSpec((tm, d), lambda i: (i, 0))
    gain = pl.BlockSpec((1, d), lambda i: (0, 0))
    more = next_g is not None
    out = pl.pallas_call(
        _out_proj_kernel,
        grid=(m // tm,),
        in_specs=[rows,
                  pl.BlockSpec((None, d, d), lambda i: (layer, 0, 0), pipeline_mode=pl.Buffered(1)),
                  rows, gain] + [gain] * more,
        out_specs=[rows] + [rows] * more,
        out_shape=[jax.ShapeDtypeStruct((m, d), F32)] + [jax.ShapeDtypeStruct((m, d), BF16)] * more,
        compiler_params=pltpu.CompilerParams(
            dimension_semantics=("parallel",), vmem_limit_bytes=VMEM_LIMIT),
        name="out_proj",
    )(merged, w_all, x2, g.reshape(1, d), *([next_g.reshape(1, d)] if more else []))
    return (out[0], out[1]) if more else (out[0], None)


def _rotary_tables(seq):
    half = HEAD_DIM // 2
    inv_freq = ROPE_BASE ** (-jnp.arange(half, dtype=F32) / half)
    ang = jnp.arange(seq, dtype=F32)[:, None] * inv_freq[None, :]
    cos, sin = jnp.cos(ang), jnp.sin(ang)
    return jnp.concatenate([cos, cos], axis=-1), jnp.concatenate([-sin, sin], axis=-1)


def kernel(x, pre_norm_g, post_norm_g, w_in, conv_w, conv_b, w_branch_a, w_branch_c, w_branch_r, w_out):
    batch, seq, d = x.shape
    assert d == D_MODEL and w_in.shape[-1] == N_IN
    assert seq % (max(DILATIONS) * ATTN_TILE) == 0 and seq % RET_CHUNK == 0
    cos2, sin2 = _rotary_tables(seq)
    log_gamma = jnp.log1p(-jnp.exp2(-5.0 - jnp.arange(N_HEADS, dtype=F32)))
    x2 = x.reshape(batch * seq, d)
    depth = w_in.shape[0]
    h = _pre_norm(x2, pre_norm_g[0])
    for layer in range(depth):
        attn_nat, attn_cm = _attn_proj(h, w_in, layer, batch, seq)
        proj = _in_proj(h, w_in, layer, REST_SEGMENTS)
        oa = _attention(attn_nat, attn_cm, proj, batch, seq)
        orr = _retention(proj, log_gamma, cos2, sin2, batch, seq)
        oc = _conv_proj(h, w_in, conv_w[layer], conv_b[layer], layer, seq)
        merged = _merge(proj, oa, oc, orr, w_branch_a, w_branch_c, w_branch_r, layer)
        next_g = pre_norm_g[layer + 1] if layer + 1 < depth else None
        x2, h = _out_proj(merged, w_out, layer, x2, post_norm_g[layer], next_g)
    return x2.reshape(batch, seq, d)
```

```python
import functools
import math

import jax
import jax.numpy as jnp
from jax import lax
from jax.experimental import pallas as pl
from jax.experimental.pallas import tpu as pltpu

F32 = jnp.float32
BF16 = jnp.bfloat16

LANE = 128
D_MODEL = 2048
HEAD_DIM = 128
N_HEADS = 12
WIDTH = N_HEADS * HEAD_DIM
N_IN = 12 * WIDTH + 3 * D_MODEL
DILATIONS = (1, 4, 16)
ATTN_TILE = 128
ATTN_BATCH = {1: 32, 4: 32, 16: 32}
LOG2_E = math.log2(math.e)
RET_CHUNK = 128
RET_BATCH = 32
CONV_K = 3
ROPE_BASE = 10000.0
NORM_EPS = 1e-6
NEG_BIG = -1e30

COL_QA, COL_KA, COL_VA, COL_ZA = 0, 12, 24, 36
COL_UC, COL_BC, COL_CC, COL_ZC = 48, 60, 72, 84
COL_QR, COL_KR, COL_VR, COL_ZR = 96, 108, 120, 132
COL_GA, COL_GC, COL_GR = 144, 160, 176
ATTN_SLABS = COL_UC - COL_QA
SLAB_QA, SLAB_KA, SLAB_VA, SLAB_ZA = COL_QA, COL_KA, COL_VA, COL_ZA
SLAB_QR, SLAB_KR, SLAB_VR, SLAB_ZR = (c - COL_QR for c in (COL_QR, COL_KR, COL_VR, COL_ZR))
SLAB_GA, SLAB_GC, SLAB_GR = (c - COL_QR for c in (COL_GA, COL_GC, COL_GR))
N_CLASSES = 4

VMEM_LIMIT = 56 * 1024 * 1024


def _sigmoid(x):
    return 0.5 * jnp.tanh(0.5 * x) + 0.5


def _silu(x):
    return x * _sigmoid(x)


def _dot(a, b):
    return jnp.dot(a, b, preferred_element_type=F32)


def _pre_norm_kernel(x_ref, g_ref, o_ref):
    xf = x_ref[...]
    ms = jnp.mean(xf * xf, axis=-1, keepdims=True)
    o_ref[...] = (xf * lax.rsqrt(ms + NORM_EPS) * g_ref[...]).astype(BF16)


def _pre_norm(x2, g, tm=512):
    m, d = x2.shape
    return pl.pallas_call(
        _pre_norm_kernel,
        grid=(m // tm,),
        in_specs=[pl.BlockSpec((tm, d), lambda i: (i, 0)), pl.BlockSpec((1, d), lambda i: (0, 0))],
        out_specs=pl.BlockSpec((tm, d), lambda i: (i, 0)),
        out_shape=jax.ShapeDtypeStruct((m, d), BF16),
        compiler_params=pltpu.CompilerParams(
            dimension_semantics=("parallel",), vmem_limit_bytes=VMEM_LIMIT),
        name="pre_norm",
    )(x2, g.reshape(1, d))


def _in_proj_kernel(h_ref, w_ref, o_ref):
    acc = _dot(h_ref[...], w_ref[...].astype(BF16))
    for c in range(o_ref.shape[0]):
        o_ref[c] = acc[:, c * LANE:(c + 1) * LANE].astype(BF16)


def _in_proj(h, w_all, layer, col0, n_slabs, tm=2048, tn=1024):
    m, d = h.shape
    tile0 = col0 * LANE // tn
    return pl.pallas_call(
        _in_proj_kernel,
        grid=(n_slabs * LANE // tn, m // tm),
        in_specs=[
            pl.BlockSpec((tm, d), lambda j, i: (i, 0)),
            pl.BlockSpec((None, d, tn), lambda j, i: (layer, 0, tile0 + j)),
        ],
        out_specs=pl.BlockSpec((tn // LANE, tm, LANE), lambda j, i: (j, i, 0)),
        out_shape=jax.ShapeDtypeStruct((n_slabs, m, LANE), BF16),
        compiler_params=pltpu.CompilerParams(
            dimension_semantics=("parallel", "parallel"), vmem_limit_bytes=VMEM_LIMIT),
        name="in_proj",
    )(h, w_all)


def _attn_proj_kernel(h_ref, w_ref, nat_ref, cm_ref, scr_ref):
    j = pl.program_id(0)
    n_slabs, n_cls, rows = cm_ref.shape[0], cm_ref.shape[1], cm_ref.shape[2]
    pair = 2 * LANE
    for c0 in range(0, n_slabs, 2):
        acc = _dot(h_ref[...], w_ref[:, c0 * LANE:c0 * LANE + pair].astype(BF16))
        for c in (c0, c0 + 1):
            is_q = (j * n_slabs + c) < N_HEADS
            slab = (acc[:, (c - c0) * LANE:(c - c0 + 1) * LANE]
                    * jnp.where(is_q, HEAD_DIM ** -0.5 * LOG2_E, 1.0))
            nat_ref[c] = slab.astype(BF16)
            scr_ref[c] = slab
            for r in range(n_cls):
                cm_ref[c, r] = scr_ref[c, pl.ds(r, rows, stride=n_cls), :]


def _attn_proj(h, w_all, layer, batch, seq, tm=1024, tn=1024):
    m, d = h.shape
    tiles_per_seq = seq // tm
    slabs = tn // LANE
    return pl.pallas_call(
        _attn_proj_kernel,
        grid=(ATTN_SLABS * LANE // tn, m // tm),
        in_specs=[
            pl.BlockSpec((tm, d), lambda j, i: (i, 0)),
            pl.BlockSpec((None, d, tn), lambda j, i: (layer, 0, COL_QA * LANE // tn + j)),
        ],
        out_specs=[
            pl.BlockSpec((slabs, tm, LANE), lambda j, i: (j, i, 0)),
            pl.BlockSpec((slabs, None, N_CLASSES, tm // N_CLASSES, LANE),
                         lambda j, i: (j, i // tiles_per_seq, 0, i % tiles_per_seq, 0)),
        ],
        out_shape=[jax.ShapeDtypeStruct((ATTN_SLABS, m, LANE), BF16),
                   jax.ShapeDtypeStruct((ATTN_SLABS, batch, N_CLASSES, seq // N_CLASSES, LANE), F32)],
        scratch_shapes=[pltpu.VMEM((slabs, tm, LANE), F32)],
        compiler_params=pltpu.CompilerParams(
            dimension_semantics=("parallel", "parallel"), vmem_limit_bytes=VMEM_LIMIT),
        name="attn_proj",
    )(h, w_all)


def _conv_proj_kernel(h_ref, wu_ref, wb_ref, wc_ref, wz_ref, cw_ref, cb_ref, o_ref, halo_ref,
                      *, tiles_per_seq):
    i = pl.program_id(1)
    tm = h_ref.shape[0]

    @pl.when(i == 0)
    def _():
        halo_ref[...] = jnp.zeros(halo_ref.shape, F32)

    h = h_ref[...]
    cu = _dot(h, wc_ref[...].astype(BF16)) * _dot(h, wu_ref[...].astype(BF16))
    halo = jnp.where((i % tiles_per_seq) == 0, 0.0, halo_ref[...])
    halo_ref[...] = cu[tm - 8:, :]
    row = lax.broadcasted_iota(jnp.int32, (8, cu.shape[1]), 0)
    back1 = pltpu.roll(cu, 1, 0)
    back2 = pltpu.roll(cu, 2, 0)
    head1 = jnp.where(row == 0, halo[7:8, :], back1[:8])
    head2 = jnp.where(row == 0, halo[6:7, :], jnp.where(row == 1, halo[7:8, :], back2[:8]))
    back1 = jnp.concatenate([head1, back1[8:]], axis=0)
    back2 = jnp.concatenate([head2, back2[8:]], axis=0)
    conv = (cb_ref[...] + cw_ref[2:3, :] * cu + cw_ref[0:1, :] * back2 + cw_ref[1:2, :] * back1)
    hz = 0.5 * _dot(h, wz_ref[...].astype(BF16))
    gated = (conv * (_dot(h, wb_ref[...].astype(BF16)) * hz)) * (1.0 + jnp.tanh(hz))
    o_ref[...] = gated.astype(BF16)


def _conv_proj(h, w_all, conv_w, conv_b, layer, seq, tm=2048, tc=256):
    m, d = h.shape

    def wspec(col):
        return pl.BlockSpec((None, d, tc), lambda jc, i: (layer, 0, col * LANE // tc + jc))

    return pl.pallas_call(
        functools.partial(_conv_proj_kernel, tiles_per_seq=seq // tm),
        grid=(WIDTH // tc, m // tm),
        in_specs=[pl.BlockSpec((tm, d), lambda jc, i: (i, 0)),
                  wspec(COL_UC), wspec(COL_BC), wspec(COL_CC), wspec(COL_ZC),
                  pl.BlockSpec((CONV_K, tc), lambda jc, i: (0, jc)),
                  pl.BlockSpec((1, tc), lambda jc, i: (0, jc))],
        out_specs=pl.BlockSpec((tm, tc), lambda jc, i: (i, jc)),
        out_shape=jax.ShapeDtypeStruct((m, WIDTH), BF16),
        scratch_shapes=[pltpu.VMEM((8, tc), F32)],
        compiler_params=pltpu.CompilerParams(
            dimension_semantics=("parallel", "arbitrary"), vmem_limit_bytes=VMEM_LIMIT),
        name="conv_proj",
    )(h, w_all, w_all, w_all, w_all, conv_w, conv_b.reshape(1, WIDTH))


def _attn_kernel(qn_ref, kn_ref, vn_ref, zn_ref, qc_ref, kc_ref, vc_ref, o_ref,
                 o_dil, lse_dil, o_far, lse_far):
    seq = qn_ref.shape[1]
    t = ATTN_TILE
    e = HEAD_DIM
    n_cls = qc_ref.shape[0]
    row = lax.broadcasted_iota(jnp.int32, (t, t), 0)
    col = lax.broadcasted_iota(jnp.int32, (t, t), 1)

    def bdot(a, b, contract_b):
        return lax.dot_general(a, b, (((2,), (contract_b,)), ((0,), (0,))),
                               preferred_element_type=F32)

    def partials(q, k_cur, k_prev, v_cur, v_prev, cur_ok, prev_ok):
        g = q.shape[0]
        keys = jnp.concatenate([k_prev, k_cur], axis=1)
        vals = jnp.concatenate([v_prev, v_cur], axis=1)
        ok = jnp.concatenate([jnp.broadcast_to(prev_ok, (g, t, t)),
                              jnp.broadcast_to(cur_ok, (g, t, t))], axis=2)
        s = jnp.where(ok, bdot(q, keys, 2), NEG_BIG)
        m = jnp.max(s, axis=-1, keepdims=True)
        p = jnp.exp2(s - m).astype(BF16)
        acc_l = bdot(p, jnp.concatenate([vals, jnp.ones(vals.shape, BF16)], axis=2), 1)
        return acc_l[:, :, :e], m, acc_l[:, :, e:]

    def previous_tiles(x):
        return jnp.concatenate([x[:, :1], x[:, :-1]], axis=1)

    def dilated_pass(dil, finish):
        step = dil // n_cls
        tiles = seq // (dil * t)
        classes = min(ATTN_BATCH[dil] // tiles, dil)
        g = classes * tiles
        slot = lax.broadcasted_iota(jnp.int32, (g, t, t), 0)
        cur_ok = col <= row
        prev_ok = jnp.logical_and(col >= row, slot % tiles > 0)

        def gather(ref, r0):
            return jnp.stack([load_class(ref, r0 + c, step, tiles * t).astype(BF16).reshape(tiles, t, e)
                              for c in range(classes)], axis=0)

        def body(i, carry):
            r0 = i * classes
            k, v = gather(kc_ref, r0), gather(vc_ref, r0)
            acc, m, l = partials(gather(qc_ref, r0).reshape(g, t, e),
                                 k.reshape(g, t, e), previous_tiles(k).reshape(g, t, e),
                                 v.reshape(g, t, e), previous_tiles(v).reshape(g, t, e),
                                 cur_ok, prev_ok)
            m = jnp.broadcast_to(m, acc.shape)
            for c in range(classes):
                part = lambda x: x[c * tiles:(c + 1) * tiles].reshape(tiles * t, e)
                finish(r0 + c, part(acc), part(m), part(l))
            return carry

        lax.fori_loop(0, dil // classes, body, 0)

    def load_class(ref, r, step, n):
        if step == 1:
            return ref[r]
        return ref[r % n_cls, pl.ds(r // n_cls, n, stride=step), :]

    def store_class(ref, r, step, n, value):
        if step == 1:
            ref[r] = value
        else:
            ref[r % n_cls, pl.ds(r // n_cls, n, stride=step), :] = value

    far = DILATIONS[2]
    near = DILATIONS[1]
    assert near == n_cls

    def finish_far(r, acc, m, l):
        n = acc.shape[0]
        store_class(o_far, r, far // n_cls, n, acc / l)
        store_class(lse_far, r, far // n_cls, n, m + jnp.log2(l))

    def finish_near(r, acc, m, l):
        lse_f = lse_far[r]
        top = jnp.maximum(m, lse_f)
        w_n = jnp.exp2(m - top)
        w_f = jnp.exp2(lse_f - top)
        den = w_n * l + w_f
        dst = pl.ds(r, acc.shape[0], stride=near)
        o_dil[dst, :] = (w_n * acc + w_f * o_far[r]) / den
        lse_dil[dst, :] = top + jnp.log2(den)

    dilated_pass(far, finish_far)
    dilated_pass(near, finish_near)

    g = min(ATTN_BATCH[1], seq // t)
    slot = lax.broadcasted_iota(jnp.int32, (g, t, t), 0)
    for i in range(seq // (g * t)):
        start = i * g * t
        rows = slice(start, start + g * t)
        if i == 0:
            with_prev = lambda ref: jnp.concatenate([ref[0, :t], ref[0, rows]], axis=0)
            prev_ok = jnp.logical_and(col >= row, slot > 0)
        else:
            with_prev = lambda ref: ref[0, start - t:start + g * t]
            prev_ok = col >= row
        k_all, v_all = with_prev(kn_ref), with_prev(vn_ref)
        acc1, m1, l1 = partials(qn_ref[0, rows].reshape(g, t, e),
                                k_all[t:].reshape(g, t, e), k_all[:g * t].reshape(g, t, e),
                                v_all[t:].reshape(g, t, e), v_all[:g * t].reshape(g, t, e),
                                col <= row, prev_ok)
        load = lambda ref: ref[rows, :].reshape(g, t, e)
        lse_d = load(lse_dil)
        m = jnp.maximum(m1, lse_d)
        w1 = jnp.exp2(m1 - m)
        w_d = jnp.exp2(lse_d - m)
        num = w1 * acc1 + w_d * load(o_dil)
        den = w1 * l1 + w_d
        z = zn_ref[0, rows].astype(F32).reshape(g, t, e)
        o_ref[rows, :] = (num / den * _silu(z)).astype(BF16).reshape(g * t, e)


def _attention(nat, cm, batch, seq):
    m = nat.shape[1]

    def spec(base):
        return pl.BlockSpec((1, seq, LANE), lambda b, h: (base + h, b, 0))

    def cm_spec(base):
        return pl.BlockSpec((None, None, N_CLASSES, seq // N_CLASSES, LANE),
                            lambda b, h: (base + h, b, 0, 0, 0))

    return pl.pallas_call(
        _attn_kernel,
        grid=(batch, N_HEADS),
        in_specs=[spec(SLAB_QA), spec(SLAB_KA), spec(SLAB_VA), spec(SLAB_ZA),
                  cm_spec(SLAB_QA), cm_spec(SLAB_KA), cm_spec(SLAB_VA)],
        out_specs=pl.BlockSpec((seq, LANE), lambda b, h: (b, h)),
        out_shape=jax.ShapeDtypeStruct((m, WIDTH), BF16),
        scratch_shapes=([pltpu.VMEM((seq, LANE), F32)] * 2
                        + [pltpu.VMEM((N_CLASSES, seq // N_CLASSES, LANE), F32)] * 2),
        compiler_params=pltpu.CompilerParams(
            dimension_semantics=("parallel", "parallel"), vmem_limit_bytes=VMEM_LIMIT),
        name="dilated_attn",
    )(nat, nat, nat, nat, cm, cm, cm)


def _ret_kernel(lg_ref, q_ref, k_ref, v_ref, z_ref, cos_ref, sin_ref, o_ref):
    seq = q_ref.shape[1]
    c = RET_CHUNK
    lg = lg_ref[pl.program_id(1)]
    row = lax.broadcasted_iota(jnp.int32, (c, c), 0).astype(F32)
    col = lax.broadcasted_iota(jnp.int32, (c, c), 1).astype(F32)
    rel = row - col
    key_scale = HEAD_DIM ** -0.5
    decay_mask = jnp.where(rel >= 0, jnp.exp(jnp.maximum(rel, 0.0) * lg), 0.0) * key_scale
    q_decay = jnp.exp((row + 1.0) * lg)
    k_decay = jnp.exp((c - 1.0 - row) * lg) * key_scale
    chunk_decay = jnp.exp(jnp.full((c, c), float(c), F32) * lg)

    g = min(RET_BATCH, seq // c)
    e = HEAD_DIM

    def bdot(a, b, contract_a, contract_b):
        return lax.dot_general(a, b, (((contract_a,), (contract_b,)), ((0,), (0,))),
                               preferred_element_type=F32)

    def body(i, state):
        sl = pl.ds(pl.multiple_of(i * (g * c), g * c), g * c)
        cos, sin = cos_ref[sl, :], sin_ref[sl, :]
        rotate = lambda x: x * cos + pltpu.roll(x, e // 2, 1) * sin
        q = rotate(q_ref[0, sl, :].astype(F32)).reshape(g, c, e)
        k = rotate(k_ref[0, sl, :].astype(F32)).reshape(g, c, e)
        v = v_ref[0, sl, :].reshape(g, c, e)
        inner = bdot(q.astype(BF16), k.astype(BF16), 2, 2) * decay_mask
        kv = bdot((k * k_decay).astype(BF16), v, 1, 1)
        states = []
        for j in range(g):
            states.append(state)
            state = state * chunk_decay + kv[j]
        lhs = jnp.concatenate([inner.astype(BF16), (q * q_decay).astype(BF16)], axis=2)
        rhs = jnp.concatenate([v, jnp.stack(states).astype(BF16)], axis=1)
        o = bdot(lhs, rhs, 2, 1)
        o = o * lax.rsqrt(jnp.mean(o * o, axis=-1, keepdims=True) + NORM_EPS)
        z = z_ref[0, sl, :].astype(F32).reshape(g, c, e)
        o_ref[sl, :] = (o * _silu(z)).astype(BF16).reshape(g * c, e)
        return state

    lax.fori_loop(0, seq // (g * c), body, jnp.zeros((e, e), F32))


def _retention(proj, log_gamma, cos2, sin2, batch, seq):
    m = proj.shape[1]

    def spec(base):
        return pl.BlockSpec((1, seq, LANE), lambda b, h, lg: (base + h, b, 0))

    table = pl.BlockSpec((seq, LANE), lambda b, h, lg: (0, 0))
    return pl.pallas_call(
        _ret_kernel,
        grid_spec=pltpu.PrefetchScalarGridSpec(
            num_scalar_prefetch=1,
            grid=(batch, N_HEADS),
            in_specs=[spec(SLAB_QR), spec(SLAB_KR), spec(SLAB_VR), spec(SLAB_ZR), table, table],
            out_specs=pl.BlockSpec((seq, LANE), lambda b, h, lg: (b, h)),
        ),
        out_shape=jax.ShapeDtypeStruct((m, WIDTH), BF16),
        compiler_params=pltpu.CompilerParams(
            dimension_semantics=("parallel", "parallel"), vmem_limit_bytes=VMEM_LIMIT),
        name="retention",
    )(log_gamma, proj, proj, proj, proj, cos2, sin2)


def _conv_ret_kernel(lg_ref, h_ref, wu_ref, wb_ref, wc_ref, wz_ref, cw_ref, cb_ref,
                     q_ref, k_ref, v_ref, z_ref, cos_ref, sin_ref, oc_ref, or_ref,
                     halo_ref, state_ref, *, tiles_per_seq, halves):
    i = pl.program_id(1)
    step = pl.program_id(0) * pl.num_programs(1) + i
    part = step % halves
    tm = h_ref.shape[0]
    c = RET_CHUNK
    e = HEAD_DIM
    g = q_ref.shape[1] // (c * halves)

    lg = lg_ref[(step // halves) % N_HEADS]
    row = lax.broadcasted_iota(jnp.int32, (c, c), 0).astype(F32)
    col = lax.broadcasted_iota(jnp.int32, (c, c), 1).astype(F32)
    rel = row - col
    key_scale = HEAD_DIM ** -0.5
    decay_mask = jnp.where(rel >= 0, jnp.exp(jnp.maximum(rel, 0.0) * lg), 0.0) * key_scale
    q_decay = jnp.exp((row + 1.0) * lg)
    k_decay = jnp.exp((c - 1.0 - row) * lg) * key_scale
    chunk_decay = jnp.exp(jnp.full((c, c), float(c), F32) * lg)

    def bdot(a, b, contract_a, contract_b):
        return lax.dot_general(a, b, (((contract_a,), (contract_b,)), ((0,), (0,))),
                               preferred_element_type=F32)

    @pl.when(i == 0)
    def _():
        halo_ref[...] = jnp.zeros(halo_ref.shape, F32)

    @pl.when(part == 0)
    def _():
        state_ref[...] = jnp.zeros(state_ref.shape, F32)

    h = h_ref[...]
    cu = _dot(h, wc_ref[...].astype(BF16)) * _dot(h, wu_ref[...].astype(BF16))

    sl = pl.ds(pl.multiple_of(part * (g * c), g * c), g * c)
    cos, sin = cos_ref[sl, :], sin_ref[sl, :]
    rotate = lambda x: x * cos + pltpu.roll(x, e // 2, 1) * sin
    q = rotate(q_ref[0, sl, :].astype(F32)).reshape(g, c, e)
    k = rotate(k_ref[0, sl, :].astype(F32)).reshape(g, c, e)
    v = v_ref[0, sl, :].reshape(g, c, e)
    inner = bdot(q.astype(BF16), k.astype(BF16), 2, 2) * decay_mask
    kv = bdot((k * k_decay).astype(BF16), v, 1, 1)

    hz = 0.5 * _dot(h, wz_ref[...].astype(BF16))

    state = state_ref[...]
    states = []
    for j in range(g):
        states.append(state)
        state = state * chunk_decay + kv[j]
    state_ref[...] = state
    lhs = jnp.concatenate([inner.astype(BF16), (q * q_decay).astype(BF16)], axis=2)
    rhs = jnp.concatenate([v, jnp.stack(states).astype(BF16)], axis=1)
    o = bdot(lhs, rhs, 2, 1)

    b_proj = _dot(h, wb_ref[...].astype(BF16))

    o = o * lax.rsqrt(jnp.mean(o * o, axis=-1, keepdims=True) + NORM_EPS)
    z = z_ref[0, sl, :].astype(F32).reshape(g, c, e)
    or_ref[sl, :] = (o * _silu(z)).astype(BF16).reshape(g * c, e)

    halo = jnp.where((i % tiles_per_seq) == 0, 0.0, halo_ref[...])
    halo_ref[...] = cu[tm - 8:, :]
    row8 = lax.broadcasted_iota(jnp.int32, (8, cu.shape[1]), 0)
    back1 = pltpu.roll(cu, 1, 0)
    back2 = pltpu.roll(cu, 2, 0)
    head1 = jnp.where(row8 == 0, halo[7:8, :], back1[:8])
    head2 = jnp.where(row8 == 0, halo[6:7, :], jnp.where(row8 == 1, halo[7:8, :], back2[:8]))
    back1 = jnp.concatenate([head1, back1[8:]], axis=0)
    back2 = jnp.concatenate([head2, back2[8:]], axis=0)
    conv = (cb_ref[...] + cw_ref[2:3, :] * cu + cw_ref[0:1, :] * back2 + cw_ref[1:2, :] * back1)
    oc_ref[...] = ((conv * (b_proj * hz)) * (1.0 + jnp.tanh(hz))).astype(BF16)


def _conv_and_retention(h, w_all, conv_w, conv_b, proj, log_gamma, cos2, sin2, layer, batch, seq,
                        tm=1024, tc=256):
    m, d = h.shape
    row_tiles = m // tm
    n_steps = (WIDTH // tc) * row_tiles
    halves = n_steps // (batch * N_HEADS)
    assert halves * batch * N_HEADS == n_steps and (seq // RET_CHUNK) % halves == 0
    unit = lambda jc, i: (jc * row_tiles + i) // halves

    def wspec(col):
        return pl.BlockSpec((None, d, tc), lambda jc, i, lg: (layer, 0, col * LANE // tc + jc))

    def rspec(base):
        return pl.BlockSpec((1, seq, LANE),
                            lambda jc, i, lg: (base + unit(jc, i) % N_HEADS, unit(jc, i) // N_HEADS, 0))

    table = pl.BlockSpec((seq, LANE), lambda jc, i, lg: (0, 0), pipeline_mode=pl.Buffered(1))
    return pl.pallas_call(
        functools.partial(_conv_ret_kernel, tiles_per_seq=seq // tm, halves=halves),
        grid_spec=pltpu.PrefetchScalarGridSpec(
            num_scalar_prefetch=1,
            grid=(WIDTH // tc, row_tiles),
            in_specs=[pl.BlockSpec((tm, d), lambda jc, i, lg: (i, 0)),
                      wspec(COL_UC), wspec(COL_BC), wspec(COL_CC), wspec(COL_ZC),
                      pl.BlockSpec((CONV_K, tc), lambda jc, i, lg: (0, jc)),
                      pl.BlockSpec((1, tc), lambda jc, i, lg: (0, jc)),
                      rspec(SLAB_QR), rspec(SLAB_KR), rspec(SLAB_VR), rspec(SLAB_ZR), table, table],
            out_specs=[pl.BlockSpec((tm, tc), lambda jc, i, lg: (i, jc)),
                       pl.BlockSpec((seq, LANE),
                                    lambda jc, i, lg: (unit(jc, i) // N_HEADS, unit(jc, i) % N_HEADS))],
            scratch_shapes=[pltpu.VMEM((8, tc), F32), pltpu.VMEM((HEAD_DIM, HEAD_DIM), F32)],
        ),
        out_shape=[jax.ShapeDtypeStruct((m, WIDTH), BF16), jax.ShapeDtypeStruct((m, WIDTH), BF16)],
        compiler_params=pltpu.CompilerParams(
            dimension_semantics=("arbitrary", "arbitrary"), vmem_limit_bytes=VMEM_LIMIT),
        name="conv_and_retention",
    )(log_gamma, h, w_all, w_all, w_all, w_all, conv_w, conv_b.reshape(1, WIDTH),
      proj, proj, proj, proj, cos2, sin2)


def _merge_kernel(oa_ref, oc_ref, or_ref, ga_ref, gc_ref, gr_ref, wa_ref, wc_ref, wr_ref, o_ref):
    ya = _dot(oa_ref[...], wa_ref[...].astype(BF16))
    yc = _dot(oc_ref[...], wc_ref[...].astype(BF16))
    yr = _dot(or_ref[...], wr_ref[...].astype(BF16))
    for s in range(ga_ref.shape[0]):
        lanes = slice(s * LANE, (s + 1) * LANE)
        merged = (_sigmoid(ga_ref[s].astype(F32)) * ya[:, lanes]
                  + _sigmoid(gc_ref[s].astype(F32)) * yc[:, lanes]
                  + _sigmoid(gr_ref[s].astype(F32)) * yr[:, lanes])
        o_ref[:, lanes] = merged.astype(BF16)


def _merge(proj, oa, oc, orr, wa, wc, wr, layer, tm=1024, tn=512):
    m = proj.shape[1]
    gs = tn // LANE

    def gate(base):
        return pl.BlockSpec((gs, tm, LANE), lambda n, i: (base // gs + n, i, 0))

    wspec = pl.BlockSpec((None, WIDTH, tn), lambda n, i: (layer, 0, n))
    own = pl.BlockSpec((tm, WIDTH), lambda n, i: (i, 0))
    return pl.pallas_call(
        _merge_kernel,
        grid=(D_MODEL // tn, m // tm),
        in_specs=[own, own, own, gate(SLAB_GA), gate(SLAB_GC), gate(SLAB_GR), wspec, wspec, wspec],
        out_specs=pl.BlockSpec((tm, tn), lambda n, i: (i, n)),
        out_shape=jax.ShapeDtypeStruct((m, D_MODEL), BF16),
        compiler_params=pltpu.CompilerParams(
            dimension_semantics=("parallel", "parallel"), vmem_limit_bytes=VMEM_LIMIT),
        name="branch_merge",
    )(oa, oc, orr, proj, proj, proj, wa, wc, wr)


def _out_proj_kernel(m_ref, w_ref, x_ref, g_ref, *rest):
    y = _dot(m_ref[...], w_ref[...].astype(BF16))
    ms = jnp.mean(y * y, axis=-1, keepdims=True)
    x_new = x_ref[...] + y * lax.rsqrt(ms + NORM_EPS) * g_ref[...]
    if len(rest) == 1:
        (o_ref,) = rest
    else:
        next_g_ref, o_ref, h_ref = rest
        ms = jnp.mean(x_new * x_new, axis=-1, keepdims=True)
        h_ref[...] = (x_new * lax.rsqrt(ms + NORM_EPS) * next_g_ref[...]).astype(BF16)
    o_ref[...] = x_new


def _out_proj(merged, w_all, layer, x2, g, next_g=None, tm=512):
    m, d = x2.shape
    rows = pl.BlockSpec((tm, d), lambda i: (i, 0))
    gain = pl.BlockSpec((1, d), lambda i: (0, 0))
    more = next_g is not None
    out = pl.pallas_call(
        _out_proj_kernel,
        grid=(m // tm,),
        in_specs=[rows,
                  pl.BlockSpec((None, d, d), lambda i: (layer, 0, 0), pipeline_mode=pl.Buffered(1)),
                  rows, gain] + [gain] * more,
        out_specs=[rows] + [rows] * more,
        out_shape=[jax.ShapeDtypeStruct((m, d), F32)] + [jax.ShapeDtypeStruct((m, d), BF16)] * more,
        compiler_params=pltpu.CompilerParams(
            dimension_semantics=("parallel",), vmem_limit_bytes=VMEM_LIMIT),
        name="out_proj",
    )(merged, w_all, x2, g.reshape(1, d), *([next_g.reshape(1, d)] if more else []))
    return (out[0], out[1]) if more else (out[0], None)


def _rotary_tables(seq):
    half = HEAD_DIM // 2
    inv_freq = ROPE_BASE ** (-jnp.arange(half, dtype=F32) / half)
    ang = jnp.arange(seq, dtype=F32)[:, None] * inv_freq[None, :]
    cos, sin = jnp.cos(ang), jnp.sin(ang)
    return jnp.concatenate([cos, cos], axis=-1), jnp.concatenate([-sin, sin], axis=-1)


def kernel(x, pre_norm_g, post_norm_g, w_in, conv_w, conv_b, w_branch_a, w_branch_c, w_branch_r, w_out):
    batch, seq, d = x.shape
    assert d == D_MODEL and w_in.shape[-1] == N_IN
    assert seq % (max(DILATIONS) * ATTN_TILE) == 0 and seq % RET_CHUNK == 0
    cos2, sin2 = _rotary_tables(seq)
    log_gamma = jnp.log1p(-jnp.exp2(-5.0 - jnp.arange(N_HEADS, dtype=F32)))
    x2 = x.reshape(batch * seq, d)
    depth = w_in.shape[0]
    h = _pre_norm(x2, pre_norm_g[0])
    for layer in range(depth):
        attn_nat, attn_cm = _attn_proj(h, w_in, layer, batch, seq)
        oa = _attention(attn_nat, attn_cm, batch, seq)
        proj = _in_proj(h, w_in, layer, COL_QR, N_IN // LANE - COL_QR)
        oc, orr = _conv_and_retention(h, w_in, conv_w[layer], conv_b[layer], proj, log_gamma,
                                      cos2, sin2, layer, batch, seq)
        merged = _merge(proj, oa, oc, orr, w_branch_a, w_branch_c, w_branch_r, layer)
        next_g = pre_norm_g[layer + 1] if layer + 1 < depth else None
        x2, h = _out_proj(merged, w_out, layer, x2, post_norm_g[layer], next_g)
    return x2.reshape(batch, seq, d)
```

```python
import functools
import math

import jax
import jax.numpy as jnp
from jax import lax
from jax.experimental import pallas as pl
from jax.experimental.pallas import tpu as pltpu

F32 = jnp.float32
BF16 = jnp.bfloat16

LANE = 128
D_MODEL = 2048
HEAD_DIM = 128
N_HEADS = 12
WIDTH = N_HEADS * HEAD_DIM
N_IN = 12 * WIDTH + 3 * D_MODEL
DILATIONS = (1, 4, 16)
ATTN_TILE = 128
ATTN_BATCH = {1: 32, 4: 32, 16: 32}
LOG2_E = math.log2(math.e)
RET_CHUNK = 128
CONV_K = 3
ROPE_BASE = 10000.0
NORM_EPS = 1e-6
NEG_BIG = -1e30

COL_QA, COL_KA, COL_VA, COL_ZA = 0, 12, 24, 36
COL_UC, COL_BC, COL_CC, COL_ZC = 48, 60, 72, 84
COL_QR, COL_KR, COL_VR, COL_ZR = 96, 108, 120, 132
COL_GA, COL_GC, COL_GR = 144, 160, 176
ATTN_SLABS = COL_UC - COL_QA
SLAB_QA, SLAB_KA, SLAB_VA, SLAB_ZA = COL_QA, COL_KA, COL_VA, COL_ZA
SLAB_QR, SLAB_KR, SLAB_VR, SLAB_ZR = (c - COL_QR for c in (COL_QR, COL_KR, COL_VR, COL_ZR))
SLAB_GA, SLAB_GC, SLAB_GR = (c - COL_QR for c in (COL_GA, COL_GC, COL_GR))
N_CLASSES = 4

VMEM_LIMIT = 56 * 1024 * 1024


def _sigmoid(x):
    return 0.5 * jnp.tanh(0.5 * x) + 0.5


def _silu(x):
    return x * _sigmoid(x)


def _dot(a, b):
    return jnp.dot(a, b, preferred_element_type=F32)


def _pre_norm_kernel(x_ref, g_ref, o_ref):
    xf = x_ref[...]
    ms = jnp.mean(xf * xf, axis=-1, keepdims=True)
    o_ref[...] = (xf * lax.rsqrt(ms + NORM_EPS) * g_ref[...]).astype(BF16)


def _pre_norm(x2, g, tm=512):
    m, d = x2.shape
    return pl.pallas_call(
        _pre_norm_kernel,
        grid=(m // tm,),
        in_specs=[pl.BlockSpec((tm, d), lambda i: (i, 0)), pl.BlockSpec((1, d), lambda i: (0, 0))],
        out_specs=pl.BlockSpec((tm, d), lambda i: (i, 0)),
        out_shape=jax.ShapeDtypeStruct((m, d), BF16),
        compiler_params=pltpu.CompilerParams(
            dimension_semantics=("parallel",), vmem_limit_bytes=VMEM_LIMIT),
        name="pre_norm",
    )(x2, g.reshape(1, d))


def _in_proj_kernel(h_ref, w_ref, o_ref):
    acc = _dot(h_ref[...], w_ref[...].astype(BF16))
    for c in range(o_ref.shape[0]):
        o_ref[c] = acc[:, c * LANE:(c + 1) * LANE].astype(BF16)


def _in_proj(h, w_all, layer, col0, n_slabs, tm=2048, tn=1024):
    m, d = h.shape
    tile0 = col0 * LANE // tn
    return pl.pallas_call(
        _in_proj_kernel,
        grid=(n_slabs * LANE // tn, m // tm),
        in_specs=[
            pl.BlockSpec((tm, d), lambda j, i: (i, 0)),
            pl.BlockSpec((None, d, tn), lambda j, i: (layer, 0, tile0 + j)),
        ],
        out_specs=pl.BlockSpec((tn // LANE, tm, LANE), lambda j, i: (j, i, 0)),
        out_shape=jax.ShapeDtypeStruct((n_slabs, m, LANE), BF16),
        compiler_params=pltpu.CompilerParams(
            dimension_semantics=("parallel", "parallel"), vmem_limit_bytes=VMEM_LIMIT),
        name="in_proj",
    )(h, w_all)


def _attn_proj_kernel(h_ref, w_ref, nat_ref, cm_ref, scr_ref):
    j = pl.program_id(0)
    n_slabs, n_cls, rows = cm_ref.shape[0], cm_ref.shape[1], cm_ref.shape[2]
    pair = 2 * LANE
    for c0 in range(0, n_slabs, 2):
        acc = _dot(h_ref[...], w_ref[:, c0 * LANE:c0 * LANE + pair].astype(BF16))
        for c in (c0, c0 + 1):
            is_q = (j * n_slabs + c) < N_HEADS
            slab = (acc[:, (c - c0) * LANE:(c - c0 + 1) * LANE]
                    * jnp.where(is_q, HEAD_DIM ** -0.5 * LOG2_E, 1.0))
            nat_ref[c] = slab.astype(BF16)
            scr_ref[c] = slab
            for r in range(n_cls):
                cm_ref[c, r] = scr_ref[c, pl.ds(r, rows, stride=n_cls), :]


def _attn_proj(h, w_all, layer, batch, seq, tm=1024, tn=1024):
    m, d = h.shape
    tiles_per_seq = seq // tm
    slabs = tn // LANE
    return pl.pallas_call(
        _attn_proj_kernel,
        grid=(ATTN_SLABS * LANE // tn, m // tm),
        in_specs=[
            pl.BlockSpec((tm, d), lambda j, i: (i, 0)),
            pl.BlockSpec((None, d, tn), lambda j, i: (layer, 0, COL_QA * LANE // tn + j)),
        ],
        out_specs=[
            pl.BlockSpec((slabs, tm, LANE), lambda j, i: (j, i, 0)),
            pl.BlockSpec((slabs, None, N_CLASSES, tm // N_CLASSES, LANE),
                         lambda j, i: (j, i // tiles_per_seq, 0, i % tiles_per_seq, 0)),
        ],
        out_shape=[jax.ShapeDtypeStruct((ATTN_SLABS, m, LANE), BF16),
                   jax.ShapeDtypeStruct((ATTN_SLABS, batch, N_CLASSES, seq // N_CLASSES, LANE), F32)],
        scratch_shapes=[pltpu.VMEM((slabs, tm, LANE), F32)],
        compiler_params=pltpu.CompilerParams(
            dimension_semantics=("parallel", "parallel"), vmem_limit_bytes=VMEM_LIMIT),
        name="attn_proj",
    )(h, w_all)


def _attn_kernel(qn_ref, kn_ref, vn_ref, zn_ref, qc_ref, kc_ref, vc_ref, o_ref,
                 o_dil, lse_dil, o_far, lse_far):
    seq = qn_ref.shape[1]
    t = ATTN_TILE
    e = HEAD_DIM
    n_cls = qc_ref.shape[0]
    row = lax.broadcasted_iota(jnp.int32, (t, t), 0)
    col = lax.broadcasted_iota(jnp.int32, (t, t), 1)

    def bdot(a, b, contract_b):
        return lax.dot_general(a, b, (((2,), (contract_b,)), ((0,), (0,))),
                               preferred_element_type=F32)

    def partials(q, k_cur, k_prev, v_cur, v_prev, cur_ok, prev_ok):
        g = q.shape[0]
        keys = jnp.concatenate([k_prev, k_cur], axis=1)
        vals = jnp.concatenate([v_prev, v_cur], axis=1)
        ok = jnp.concatenate([jnp.broadcast_to(prev_ok, (g, t, t)),
                              jnp.broadcast_to(cur_ok, (g, t, t))], axis=2)
        s = jnp.where(ok, bdot(q, keys, 2), NEG_BIG)
        m = jnp.max(s, axis=-1, keepdims=True)
        p = jnp.exp2(s - m).astype(BF16)
        acc_l = bdot(p, jnp.concatenate([vals, jnp.ones(vals.shape, BF16)], axis=2), 1)
        return acc_l[:, :, :e], m, acc_l[:, :, e:]

    def previous_tiles(x):
        return jnp.concatenate([x[:, :1], x[:, :-1]], axis=1)

    def dilated_pass(dil, finish):
        step = dil // n_cls
        tiles = seq // (dil * t)
        classes = min(ATTN_BATCH[dil] // tiles, dil)
        g = classes * tiles
        slot = lax.broadcasted_iota(jnp.int32, (g, t, t), 0)
        cur_ok = col <= row
        prev_ok = jnp.logical_and(col >= row, slot % tiles > 0)

        def gather(ref, r0):
            return jnp.stack([load_class(ref, r0 + c, step, tiles * t).astype(BF16).reshape(tiles, t, e)
                              for c in range(classes)], axis=0)

        def body(i, carry):
            r0 = i * classes
            k, v = gather(kc_ref, r0), gather(vc_ref, r0)
            acc, m, l = partials(gather(qc_ref, r0).reshape(g, t, e),
                                 k.reshape(g, t, e), previous_tiles(k).reshape(g, t, e),
                                 v.reshape(g, t, e), previous_tiles(v).reshape(g, t, e),
                                 cur_ok, prev_ok)
            m = jnp.broadcast_to(m, acc.shape)
            for c in range(classes):
                part = lambda x: x[c * tiles:(c + 1) * tiles].reshape(tiles * t, e)
                finish(r0 + c, part(acc), part(m), part(l))
            return carry

        lax.fori_loop(0, dil // classes, body, 0)

    def load_class(ref, r, step, n):
        if step == 1:
            return ref[r]
        return ref[r % n_cls, pl.ds(r // n_cls, n, stride=step), :]

    def store_class(ref, r, step, n, value):
        if step == 1:
            ref[r] = value
        else:
            ref[r % n_cls, pl.ds(r // n_cls, n, stride=step), :] = value

    far = DILATIONS[2]
    near = DILATIONS[1]
    assert near == n_cls

    def finish_far(r, acc, m, l):
        n = acc.shape[0]
        store_class(o_far, r, far // n_cls, n, acc / l)
        store_class(lse_far, r, far // n_cls, n, m + jnp.log2(l))

    def finish_near(r, acc, m, l):
        lse_f = lse_far[r]
        top = jnp.maximum(m, lse_f)
        w_n = jnp.exp2(m - top)
        w_f = jnp.exp2(lse_f - top)
        den = w_n * l + w_f
        dst = pl.ds(r, acc.shape[0], stride=near)
        o_dil[dst, :] = (w_n * acc + w_f * o_far[r]) / den
        lse_dil[dst, :] = top + jnp.log2(den)

    dilated_pass(far, finish_far)
    dilated_pass(near, finish_near)

    g = min(ATTN_BATCH[1], seq // t)
    slot = lax.broadcasted_iota(jnp.int32, (g, t, t), 0)
    for i in range(seq // (g * t)):
        start = i * g * t
        rows = slice(start, start + g * t)
        if i == 0:
            with_prev = lambda ref: jnp.concatenate([ref[0, :t], ref[0, rows]], axis=0)
            prev_ok = jnp.logical_and(col >= row, slot > 0)
        else:
            with_prev = lambda ref: ref[0, start - t:start + g * t]
            prev_ok = col >= row
        k_all, v_all = with_prev(kn_ref), with_prev(vn_ref)
        acc1, m1, l1 = partials(qn_ref[0, rows].reshape(g, t, e),
                                k_all[t:].reshape(g, t, e), k_all[:g * t].reshape(g, t, e),
                                v_all[t:].reshape(g, t, e), v_all[:g * t].reshape(g, t, e),
                                col <= row, prev_ok)
        load = lambda ref: ref[rows, :].reshape(g, t, e)
        lse_d = load(lse_dil)
        m = jnp.maximum(m1, lse_d)
        w1 = jnp.exp2(m1 - m)
        w_d = jnp.exp2(lse_d - m)
        num = w1 * acc1 + w_d * load(o_dil)
        den = w1 * l1 + w_d
        z = zn_ref[0, rows].astype(F32).reshape(g, t, e)
        o_ref[rows, :] = (num / den * _silu(z)).astype(BF16).reshape(g * t, e)


def _attention(nat, cm, batch, seq):
    m = nat.shape[1]

    def spec(base):
        return pl.BlockSpec((1, seq, LANE), lambda b, h: (base + h, b, 0))

    def cm_spec(base):
        return pl.BlockSpec((None, None, N_CLASSES, seq // N_CLASSES, LANE),
                            lambda b, h: (base + h, b, 0, 0, 0))

    return pl.pallas_call(
        _attn_kernel,
        grid=(batch, N_HEADS),
        in_specs=[spec(SLAB_QA), spec(SLAB_KA), spec(SLAB_VA), spec(SLAB_ZA),
                  cm_spec(SLAB_QA), cm_spec(SLAB_KA), cm_spec(SLAB_VA)],
        out_specs=pl.BlockSpec((seq, LANE), lambda b, h: (b, h)),
        out_shape=jax.ShapeDtypeStruct((m, WIDTH), BF16),
        scratch_shapes=([pltpu.VMEM((seq, LANE), F32)] * 2
                        + [pltpu.VMEM((N_CLASSES, seq // N_CLASSES, LANE), F32)] * 2),
        compiler_params=pltpu.CompilerParams(
            dimension_semantics=("parallel", "parallel"), vmem_limit_bytes=VMEM_LIMIT),
        name="dilated_attn",
    )(nat, nat, nat, nat, cm, cm, cm)


def _conv_ret_kernel(lg_ref, h_ref, wu_ref, wb_ref, wc_ref, wz_ref, cw_ref, cb_ref,
                     q_ref, k_ref, v_ref, z_ref, cos_ref, sin_ref, oc_ref, or_ref,
                     halo_ref, state_ref, *, tiles_per_seq, halves):
    i = pl.program_id(1)
    step = pl.program_id(0) * pl.num_programs(1) + i
    part = step % halves
    tm = h_ref.shape[0]
    c = RET_CHUNK
    e = HEAD_DIM
    g = q_ref.shape[1] // (c * halves)

    lg = lg_ref[(step // halves) % N_HEADS]
    row = lax.broadcasted_iota(jnp.int32, (c, c), 0).astype(F32)
    col = lax.broadcasted_iota(jnp.int32, (c, c), 1).astype(F32)
    rel = row - col
    key_scale = HEAD_DIM ** -0.5
    decay_mask = jnp.where(rel >= 0, jnp.exp(jnp.maximum(rel, 0.0) * lg), 0.0) * key_scale
    q_decay = jnp.exp((row + 1.0) * lg)
    k_decay = jnp.exp((c - 1.0 - row) * lg) * key_scale
    chunk_decay = jnp.exp(jnp.full((c, c), float(c), F32) * lg)

    def bdot(a, b, contract_a, contract_b):
        return lax.dot_general(a, b, (((contract_a,), (contract_b,)), ((0,), (0,))),
                               preferred_element_type=F32)

    @pl.when(i == 0)
    def _():
        halo_ref[...] = jnp.zeros(halo_ref.shape, F32)

    @pl.when(part == 0)
    def _():
        state_ref[...] = jnp.zeros(state_ref.shape, F32)

    h = h_ref[...]
    c_proj = _dot(h, wc_ref[...].astype(BF16))

    sl = pl.ds(pl.multiple_of(part * (g * c), g * c), g * c)
    cos, sin = cos_ref[sl, :], sin_ref[sl, :]
    rotate = lambda x: x * cos + pltpu.roll(x, e // 2, 1) * sin
    q = rotate(q_ref[0, sl, :].astype(F32)).reshape(g, c, e)
    k = rotate(k_ref[0, sl, :].astype(F32)).reshape(g, c, e)
    v = v_ref[0, sl, :].reshape(g, c, e)
    inner = bdot(q.astype(BF16), k.astype(BF16), 2, 2) * decay_mask
    kv = bdot((k * k_decay).astype(BF16), v, 1, 1)

    cu = c_proj * _dot(h, wu_ref[...].astype(BF16))

    state = state_ref[...]
    states = []
    for j in range(g):
        states.append(state)
        state = state * chunk_decay + kv[j]
    state_ref[...] = state
    lhs = jnp.concatenate([inner.astype(BF16), (q * q_decay).astype(BF16)], axis=2)
    rhs = jnp.concatenate([v, jnp.stack(states).astype(BF16)], axis=1)
    o = bdot(lhs, rhs, 2, 1)

    hz = 0.5 * _dot(h, wz_ref[...].astype(BF16))
    b_proj = _dot(h, wb_ref[...].astype(BF16))

    o = o * lax.rsqrt(jnp.mean(o * o, axis=-1, keepdims=True) + NORM_EPS)
    z = z_ref[0, sl, :].astype(F32).reshape(g, c, e)
    or_ref[sl, :] = (o * _silu(z)).astype(BF16).reshape(g * c, e)

    halo = jnp.where((i % tiles_per_seq) == 0, 0.0, halo_ref[...])
    halo_ref[...] = cu[tm - 8:, :]
    row8 = lax.broadcasted_iota(jnp.int32, (8, cu.shape[1]), 0)
    back1 = pltpu.roll(cu, 1, 0)
    back2 = pltpu.roll(cu, 2, 0)
    head1 = jnp.where(row8 == 0, halo[7:8, :], back1[:8])
    head2 = jnp.where(row8 == 0, halo[6:7, :], jnp.where(row8 == 1, halo[7:8, :], back2[:8]))
    back1 = jnp.concatenate([head1, back1[8:]], axis=0)
    back2 = jnp.concatenate([head2, back2[8:]], axis=0)
    conv = (cb_ref[...] + cw_ref[2:3, :] * cu + cw_ref[0:1, :] * back2 + cw_ref[1:2, :] * back1)
    oc_ref[...] = ((conv * (b_proj * hz)) * (1.0 + jnp.tanh(hz))).astype(BF16)


def _conv_and_retention(h, w_all, conv_w, conv_b, proj, log_gamma, cos2, sin2, layer, batch, seq,
                        tm=1024, tc=256):
    m, d = h.shape
    row_tiles = m // tm
    n_steps = (WIDTH // tc) * row_tiles
    halves = n_steps // (batch * N_HEADS)
    assert halves * batch * N_HEADS == n_steps and (seq // RET_CHUNK) % halves == 0
    unit = lambda jc, i: (jc * row_tiles + i) // halves

    def wspec(col):
        return pl.BlockSpec((None, d, tc), lambda jc, i, lg: (layer, 0, col * LANE // tc + jc))

    def rspec(base):
        return pl.BlockSpec((1, seq, LANE),
                            lambda jc, i, lg: (base + unit(jc, i) % N_HEADS, unit(jc, i) // N_HEADS, 0))

    table = pl.BlockSpec((seq, LANE), lambda jc, i, lg: (0, 0), pipeline_mode=pl.Buffered(1))
    return pl.pallas_call(
        functools.partial(_conv_ret_kernel, tiles_per_seq=seq // tm, halves=halves),
        grid_spec=pltpu.PrefetchScalarGridSpec(
            num_scalar_prefetch=1,
            grid=(WIDTH // tc, row_tiles),
            in_specs=[pl.BlockSpec((tm, d), lambda jc, i, lg: (i, 0)),
                      wspec(COL_UC), wspec(COL_BC), wspec(COL_CC), wspec(COL_ZC),
                      pl.BlockSpec((CONV_K, tc), lambda jc, i, lg: (0, jc)),
                      pl.BlockSpec((1, tc), lambda jc, i, lg: (0, jc)),
                      rspec(SLAB_QR), rspec(SLAB_KR), rspec(SLAB_VR), rspec(SLAB_ZR), table, table],
            out_specs=[pl.BlockSpec((tm, tc), lambda jc, i, lg: (i, jc)),
                       pl.BlockSpec((seq, LANE),
                                    lambda jc, i, lg: (unit(jc, i) // N_HEADS, unit(jc, i) % N_HEADS))],
            scratch_shapes=[pltpu.VMEM((8, tc), F32), pltpu.VMEM((HEAD_DIM, HEAD_DIM), F32)],
        ),
        out_shape=[jax.ShapeDtypeStruct((m, WIDTH), BF16), jax.ShapeDtypeStruct((m, WIDTH), BF16)],
        compiler_params=pltpu.CompilerParams(
            dimension_semantics=("arbitrary", "arbitrary"), vmem_limit_bytes=VMEM_LIMIT),
        name="conv_and_retention",
    )(log_gamma, h, w_all, w_all, w_all, w_all, conv_w, conv_b.reshape(1, WIDTH),
      proj, proj, proj, proj, cos2, sin2)


def _merge_kernel(oa_ref, oc_ref, or_ref, ga_ref, gc_ref, gr_ref, wa_ref, wc_ref, wr_ref, o_ref):
    ya = _dot(oa_ref[...], wa_ref[...].astype(BF16))
    yc = _dot(oc_ref[...], wc_ref[...].astype(BF16))
    yr = _dot(or_ref[...], wr_ref[...].astype(BF16))
    for s in range(ga_ref.shape[0]):
        lanes = slice(s * LANE, (s + 1) * LANE)
        merged = (_sigmoid(ga_ref[s].astype(F32)) * ya[:, lanes]
                  + _sigmoid(gc_ref[s].astype(F32)) * yc[:, lanes]
                  + _sigmoid(gr_ref[s].astype(F32)) * yr[:, lanes])
        o_ref[:, lanes] = merged.astype(BF16)


def _merge(proj, oa, oc, orr, wa, wc, wr, layer, tm=1024, tn=512):
    m = proj.shape[1]
    gs = tn // LANE

    def gate(base):
        return pl.BlockSpec((gs, tm, LANE), lambda n, i: (base // gs + n, i, 0))

    wspec = pl.BlockSpec((None, WIDTH, tn), lambda n, i: (layer, 0, n))
    own = pl.BlockSpec((tm, WIDTH), lambda n, i: (i, 0))
    return pl.pallas_call(
        _merge_kernel,
        grid=(D_MODEL // tn, m // tm),
        in_specs=[own, own, own, gate(SLAB_GA), gate(SLAB_GC), gate(SLAB_GR), wspec, wspec, wspec],
        out_specs=pl.BlockSpec((tm, tn), lambda n, i: (i, n)),
        out_shape=jax.ShapeDtypeStruct((m, D_MODEL), BF16),
        compiler_params=pltpu.CompilerParams(
            dimension_semantics=("parallel", "parallel"), vmem_limit_bytes=VMEM_LIMIT),
        name="branch_merge",
    )(oa, oc, orr, proj, proj, proj, wa, wc, wr)


def _out_proj_kernel(m_ref, w_ref, x_ref, g_ref, *rest):
    y = _dot(m_ref[...], w_ref[...].astype(BF16))
    ms = jnp.mean(y * y, axis=-1, keepdims=True)
    x_new = x_ref[...] + y * lax.rsqrt(ms + NORM_EPS) * g_ref[...]
    if len(rest) == 1:
        (o_ref,) = rest
    else:
        next_g_ref, o_ref, h_ref = rest
        ms = jnp.mean(x_new * x_new, axis=-1, keepdims=True)
        h_ref[...] = (x_new * lax.rsqrt(ms + NORM_EPS) * next_g_ref[...]).astype(BF16)
    o_ref[...] = x_new


def _out_proj(merged, w_all, layer, x2, g, next_g=None, tm=512):
    m, d = x2.shape
    rows = pl.BlockSpec((tm, d), lambda i: (i, 0))
    gain = pl.BlockSpec((1, d), lambda i: (0, 0))
    more = next_g is not None
    out = pl.pallas_call(
        _out_proj_kernel,
        grid=(m // tm,),
        in_specs=[rows,
                  pl.BlockSpec((None, d, d), lambda i: (layer, 0, 0), pipeline_mode=pl.Buffered(1)),
                  rows, gain] + [gain] * more,
        out_specs=[rows] + [rows] * more,
        out_shape=[jax.ShapeDtypeStruct((m, d), F32)] + [jax.ShapeDtypeStruct((m, d), BF16)] * more,
        compiler_params=pltpu.CompilerParams(
            dimension_semantics=("parallel",), vmem_limit_bytes=VMEM_LIMIT),
        name="out_proj",
    )(merged, w_all, x2, g.reshape(1, d), *([next_g.reshape(1, d)] if more else []))
    return (out[0], out[1]) if more else (out[0], None)


def _rotary_tables(seq):
    half = HEAD_DIM // 2
    inv_freq = ROPE_BASE ** (-jnp.arange(half, dtype=F32) / half)
    ang = jnp.arange(seq, dtype=F32)[:, None] * inv_freq[None, :]
    cos, sin = jnp.cos(ang), jnp.sin(ang)
    return jnp.concatenate([cos, cos], axis=-1), jnp.concatenate([-sin, sin], axis=-1)


def kernel(x, pre_norm_g, post_norm_g, w_in, conv_w, conv_b, w_branch_a, w_branch_c, w_branch_r, w_out):
    batch, seq, d = x.shape
    assert d == D_MODEL and w_in.shape[-1] == N_IN
    assert seq % (max(DILATIONS) * ATTN_TILE) == 0 and seq % RET_CHUNK == 0
    cos2, sin2 = _rotary_tables(seq)
    log_gamma = jnp.log1p(-jnp.exp2(-5.0 - jnp.arange(N_HEADS, dtype=F32)))
    x2 = x.reshape(batch * seq, d)
    depth = w_in.shape[0]
    h = _pre_norm(x2, pre_norm_g[0])
    for layer in range(depth):
        attn_nat, attn_cm = _attn_proj(h, w_in, layer, batch, seq)
        oa = _attention(attn_nat, attn_cm, batch, seq)
        proj = _in_proj(h, w_in, layer, COL_QR, N_IN // LANE - COL_QR)
        oc, orr = _conv_and_retention(h, w_in, conv_w[layer], conv_b[layer], proj, log_gamma,
                                      cos2, sin2, layer, batch, seq)
        merged = _merge(proj, oa, oc, orr, w_branch_a, w_branch_c, w_branch_r, layer)
        next_g = pre_norm_g[layer + 1] if layer + 1 < depth else None
        x2, h = _out_proj(merged, w_out, layer, x2, post_norm_g[layer], next_g)
    return x2.reshape(batch, seq, d)
```
